```python
import math
import jax, jax.numpy as jnp
from jax import lax
import numpy as np

D_MODEL = 2048
BATCH = 2
SEQ = 4096
DEPTH = 1

HEAD_DIM = 128
MOBA_HEADS = 8
NSA_HEADS = 8
NSA_KV_HEADS = 2
NSA_GROUP = NSA_HEADS // NSA_KV_HEADS
MOBA_BLOCK = 256
MOBA_TOPK = 3
MOBA_Q_CHUNK = 64
CMP_LEN = 32
CMP_STRIDE = 16
CMP_HIDDEN = 2 * HEAD_DIM
SLC_BLOCK = 64
SLC_TOPK = 16
SLC_Q_CHUNK = 64
WINDOW = 512
SEQ_ALIGN = 512
D_FF = 4 * D_MODEL
CONV_WIDTH = 3
ROPE_THETA = 10000.0
EPS = 1e-6
NEG = -1e30
BIG = 1e9

MOBA_WIDTH = MOBA_HEADS * HEAD_DIM
NSA_WIDTH = NSA_HEADS * HEAD_DIM
KV_WIDTH = NSA_KV_HEADS * HEAD_DIM
IN_SPLITS = [MOBA_WIDTH, MOBA_WIDTH, MOBA_WIDTH, NSA_WIDTH] + [KV_WIDTH] * 6 + [3 * NSA_HEADS]
IN_COLS = 3 * MOBA_WIDTH + NSA_WIDTH + 6 * KV_WIDTH + 3 * NSA_HEADS

kernel_name = "hymba_moba_nsa_convffn_adaln"


def rmsnorm(x, w):
    xf = x.astype(jnp.float32)
    r = xf * lax.rsqrt(jnp.mean(xf * xf, axis=-1, keepdims=True) + EPS)
    return (r * w.astype(jnp.float32)).astype(x.dtype)


def rope(x, positions):
    half = x.shape[-1] // 2
    inv_freq = ROPE_THETA ** (-jnp.arange(half, dtype=jnp.float32) / half)
    ang = positions.astype(jnp.float32)[..., None] * inv_freq
    cos = jnp.cos(ang)[:, :, None, :]
    sin = jnp.sin(ang)[:, :, None, :]
    xf = x.astype(jnp.float32)
    x1, x2 = xf[..., :half], xf[..., half:]
    return jnp.concatenate([x1 * cos - x2 * sin, x2 * cos + x1 * sin], axis=-1).astype(x.dtype)


def moba_attention(q, k, v):
    B, S, H, D = q.shape
    nb = S // MOBA_BLOCK
    scale = D ** -0.5
    q = q.transpose(0, 2, 1, 3)
    kb = k.transpose(0, 2, 1, 3).reshape(B, H, nb, MOBA_BLOCK, D)
    vb = v.transpose(0, 2, 1, 3).reshape(B, H, nb, MOBA_BLOCK, D)
    kmean = jnp.mean(kb.astype(jnp.float32), axis=3)
    gate = jnp.einsum('bhsd,bhnd->bhsn', q.astype(jnp.float32), kmean)
    qblk = jnp.arange(S) // MOBA_BLOCK
    past = jnp.arange(nb)[None, :] < qblk[:, None]
    gate = jnp.where(past, gate, NEG)
    kk = min(MOBA_TOPK, nb)
    _, sel = lax.top_k(gate, kk)
    bi = jnp.arange(B)[:, None, None, None]
    hi = jnp.arange(H)[None, :, None, None]
    qc_len = MOBA_Q_CHUNK

    def chunk(ci):
        start = ci * qc_len
        qc = lax.dynamic_slice_in_dim(q, start, qc_len, axis=2)
        selc = lax.dynamic_slice_in_dim(sel, start, qc_len, axis=2)
        own = start // MOBA_BLOCK
        k_own = lax.dynamic_index_in_dim(kb, own, axis=2, keepdims=False)
        v_own = lax.dynamic_index_in_dim(vb, own, axis=2, keepdims=False)
        k_sel = kb[bi, hi, selc]
        v_sel = vb[bi, hi, selc]
        qpos = start + jnp.arange(qc_len)
        kpos_own = own * MOBA_BLOCK + jnp.arange(MOBA_BLOCK)
        s_own = jnp.einsum('bhqd,bhkd->bhqk', qc, k_own).astype(jnp.float32) * scale
        s_own = jnp.where(kpos_own[None, :] <= qpos[:, None], s_own, NEG)
        s_sel = jnp.einsum('bhqd,bhqnkd->bhqnk', qc, k_sel).astype(jnp.float32) * scale
        valid = jnp.arange(kk)[None, :] < (qpos // MOBA_BLOCK)[:, None]
        s_sel = jnp.where(valid[:, :, None], s_sel, NEG)
        s = jnp.concatenate([s_own, s_sel.reshape(B, H, qc_len, kk * MOBA_BLOCK)], axis=-1)
        p = jax.nn.softmax(s, axis=-1).astype(v.dtype)
        p_sel = p[..., MOBA_BLOCK:].reshape(B, H, qc_len, kk, MOBA_BLOCK)
        return (jnp.einsum('bhqk,bhkd->bhqd', p[..., :MOBA_BLOCK], v_own)
                + jnp.einsum('bhqnk,bhqnkd->bhqd', p_sel, v_sel))

    out = lax.map(chunk, jnp.arange(S // qc_len))
    return out.transpose(1, 0, 3, 2, 4).reshape(B, S, H * D)


def compress(x, pos_emb, w1, w2):
    B, S, Hk, D = x.shape
    nch = S // CMP_STRIDE
    r = CMP_LEN // CMP_STRIDE
    nc = nch - r + 1
    chunks = x.reshape(B, nch, CMP_STRIDE, Hk, D)
    blocks = jnp.concatenate([chunks[:, i:i + nc] for i in range(r)], axis=2)
    blocks = blocks + pos_emb[None, None, :, None, :]
    flat = blocks.transpose(0, 3, 1, 2, 4).reshape(B, Hk, nc, CMP_LEN * D)
    hid = jax.nn.gelu(jnp.einsum('bknf,fh->bknh', flat, w1), approximate=True)
    return jnp.einsum('bknh,hd->bknd', hid, w2)


def nsa_attention(q, k_cmp, v_cmp, k_slc, v_slc, k_win, v_win, gate_logits,
                  cmp_pos_k, cmp_w1_k, cmp_w2_k, cmp_pos_v, cmp_w1_v, cmp_w2_v):
    B, S, H, D = q.shape
    Hk, G = NSA_KV_HEADS, NSA_GROUP
    scale = D ** -0.5
    t = jnp.arange(S)
    qg = q.reshape(B, S, Hk, G, D).transpose(0, 2, 3, 1, 4)

    kc = compress(k_cmp, cmp_pos_k, cmp_w1_k, cmp_w2_k)
    vc = compress(v_cmp, cmp_pos_v, cmp_w1_v, cmp_w2_v)
    nc = kc.shape[2]
    cmp_end = jnp.arange(nc) * CMP_STRIDE + CMP_LEN - 1
    cmp_mask = cmp_end[None, :] <= t[:, None]
    s_cmp = jnp.einsum('bkgsd,bknd->bkgsn', qg, kc).astype(jnp.float32) * scale
    s_cmp = jnp.where(cmp_mask, s_cmp, NEG)
    p_cmp = jnp.where(cmp_mask, jax.nn.softmax(s_cmp, axis=-1), 0.0)
    o_cmp = jnp.einsum('bkgsn,bknd->bkgsd', p_cmp.astype(vc.dtype), vc)

    imp = jnp.sum(p_cmp, axis=2)
    rs, rc = SLC_BLOCK // CMP_STRIDE, CMP_LEN // CMP_STRIDE
    n_slc = S // SLC_BLOCK
    imp_pad = jnp.pad(imp, ((0, 0), (0, 0), (0, 0), (rc - 1, rs)))
    p_slc = 0.0
    for o in range(rs + rc - 1):
        w_o = float(sum(1 for m in range(rs) for n in range(rc) if m - n + rc - 1 == o))
        p_slc = p_slc + w_o * imp_pad[..., o:o + rs * n_slc:rs]
    jt = t // SLC_BLOCK
    j = jnp.arange(n_slc)
    valid = j[None, :] <= jt[:, None]
    forced = (j[None, :] == 0) | (j[None, :] == jt[:, None]) | (j[None, :] == jt[:, None] - 1)
    score = jnp.where(valid & forced, BIG, jnp.where(valid, p_slc, NEG))
    n_sel = min(SLC_TOPK, n_slc)
    _, sel = lax.top_k(score, n_sel)

    ksb = k_slc.transpose(0, 2, 1, 3).reshape(B, Hk, n_slc, SLC_BLOCK, D)
    vsb = v_slc.transpose(0, 2, 1, 3).reshape(B, Hk, n_slc, SLC_BLOCK, D)
    bi = jnp.arange(B)[:, None, None, None]
    hi = jnp.arange(Hk)[None, :, None, None]
    qn = SLC_Q_CHUNK

    def slc_chunk(ci):
        start = ci * qn
        qc = lax.dynamic_slice_in_dim(qg, start, qn, axis=3)
        selc = lax.dynamic_slice_in_dim(sel, start, qn, axis=2)
        kg = ksb[bi, hi, selc]
        vg = vsb[bi, hi, selc]
        s = jnp.einsum('bkgqd,bkqnjd->bkgqnj', qc, kg).astype(jnp.float32) * scale
        kpos = selc[..., None] * SLC_BLOCK + jnp.arange(SLC_BLOCK)
        qpos = start + jnp.arange(qn)
        mask = kpos <= qpos[None, None, :, None, None]
        s = jnp.where(mask[:, :, None], s, NEG)
        p = jax.nn.softmax(s.reshape(B, Hk, G, qn, n_sel * SLC_BLOCK), axis=-1)
        p = p.reshape(s.shape).astype(vg.dtype)
        return jnp.einsum('bkgqnj,bkqnjd->bkgqd', p, vg)

    o_slc = lax.map(slc_chunk, jnp.arange(S // qn))
    o_slc = o_slc.transpose(1, 2, 3, 0, 4, 5).reshape(B, Hk, G, S, D)

    nw = S // WINDOW
    kwb = k_win.transpose(0, 2, 1, 3).reshape(B, Hk, nw, WINDOW, D)
    vwb = v_win.transpose(0, 2, 1, 3).reshape(B, Hk, nw, WINDOW, D)
    pad5 = ((0, 0), (0, 0), (1, 0), (0, 0), (0, 0))
    kcat = jnp.concatenate([jnp.pad(kwb, pad5)[:, :, :-1], kwb], axis=3)
    vcat = jnp.concatenate([jnp.pad(vwb, pad5)[:, :, :-1], vwb], axis=3)
    qw = qg.reshape(B, Hk, G, nw, WINDOW, D)
    s_w = jnp.einsum('bkgnqd,bknjd->bkgnqj', qw, kcat).astype(jnp.float32) * scale
    qi = jnp.arange(WINDOW)[:, None]
    kj = jnp.arange(2 * WINDOW)[None, :] - WINDOW
    band = (kj <= qi) & (kj > qi - WINDOW)
    first = (jnp.arange(nw) == 0)[:, None, None]
    wmask = band[None] & (~first | (kj >= 0)[None])
    s_w = jnp.where(wmask, s_w, NEG)
    p_w = jax.nn.softmax(s_w, axis=-1).astype(vcat.dtype)
    o_win = jnp.einsum('bkgnqj,bknjd->bkgnqd', p_w, vcat).reshape(B, Hk, G, S, D)

    g = jax.nn.sigmoid(gate_logits.astype(jnp.float32)).reshape(B, S, 3, Hk, G)
    g = g.transpose(2, 0, 3, 4, 1)[..., None].astype(q.dtype)
    o = g[0] * o_cmp + g[1] * o_slc + g[2] * o_win
    return o.transpose(0, 3, 1, 2, 4).reshape(B, S, H * D)


def hybrid_mixer(h, positions, w_in, w_out, cmp_pos_k, cmp_w1_k, cmp_w2_k,
                 cmp_pos_v, cmp_w1_v, cmp_w2_v):
    B, S, _ = h.shape
    sp = -(-S // SEQ_ALIGN) * SEQ_ALIGN
    pad = sp - S
    h = jnp.pad(h, ((0, 0), (0, pad), (0, 0)))
    pos = jnp.concatenate(
        [positions, positions[:, -1:] + 1 + jnp.arange(pad, dtype=positions.dtype)[None, :]], axis=1)
    proj = jnp.einsum('bsd,de->bse', h, w_in)
    splits = np.cumsum(IN_SPLITS)[:-1].tolist()
    mq, mk, mv, nq, kc, vc, ks, vs, kw, vw, ng = jnp.split(proj, splits, axis=-1)

    def heads(a, n):
        return a.reshape(B, sp, n, HEAD_DIM)

    o_moba = moba_attention(rope(heads(mq, MOBA_HEADS), pos), rope(heads(mk, MOBA_HEADS), pos),
                            heads(mv, MOBA_HEADS))
    o_nsa = nsa_attention(rope(heads(nq, NSA_HEADS), pos),
                          rope(heads(kc, NSA_KV_HEADS), pos), heads(vc, NSA_KV_HEADS),
                          rope(heads(ks, NSA_KV_HEADS), pos), heads(vs, NSA_KV_HEADS),
                          rope(heads(kw, NSA_KV_HEADS), pos), heads(vw, NSA_KV_HEADS),
                          ng, cmp_pos_k, cmp_w1_k, cmp_w2_k, cmp_pos_v, cmp_w1_v, cmp_w2_v)
    o = jnp.concatenate([o_moba, o_nsa], axis=-1)[:, :S]
    return jnp.einsum('bse,ed->bsd', o, w_out)


def conv_ffn(h, w_up, conv_w, conv_b, w_down):
    u = jnp.einsum('bsd,df->bsf', h, w_up)
    ch = u.shape[-1]
    u = lax.conv_general_dilated(u, conv_w[:, None, :], window_strides=(1,),
                                 padding=((CONV_WIDTH - 1, 0),),
                                 dimension_numbers=('NWC', 'WIO', 'NWC'),
                                 feature_group_count=ch) + conv_b
    gate, val = jnp.split(u, 2, axis=-1)
    return jnp.einsum('bsf,fd->bsd', jax.nn.gelu(gate, approximate=True) * val, w_down)


def setup_inputs(seed: int = 0) -> dict:
    key = jax.random.key(seed)
    ks = jax.random.split(key, 24)
    f32 = jnp.float32

    def nrm(k, shape, fan_in):
        return jax.random.normal(k, shape, f32) * (fan_in ** -0.5)

    def gain(k):
        return 1.0 + 0.1 * jax.random.normal(k, (DEPTH, D_MODEL), f32)

    x = jax.random.normal(ks[0], (BATCH, SEQ, D_MODEL), f32)
    c = jax.random.normal(ks[1], (BATCH, D_MODEL), f32)
    off = jax.random.randint(ks[2], (BATCH,), 0, 1024, dtype=jnp.int32)
    positions = (off[:, None] + jnp.arange(SEQ, dtype=jnp.int32)[None, :]).astype(jnp.int32)
    return {
        "x": x,
        "c": c,
        "positions": positions,
        "w_ada": nrm(ks[3], (DEPTH, D_MODEL, 6 * D_MODEL), D_MODEL),
        "b_ada": 0.01 * jax.random.normal(ks[4], (DEPTH, 6 * D_MODEL), f32),
        "norm_pre_mix": gain(ks[5]),
        "norm_post_mix": gain(ks[6]),
        "norm_pre_ffn": gain(ks[7]),
        "norm_post_ffn": gain(ks[8]),
        "w_in": nrm(ks[9], (DEPTH, D_MODEL, IN_COLS), D_MODEL),
        "w_out": nrm(ks[10], (DEPTH, D_MODEL, D_MODEL), D_MODEL),
        "cmp_pos_k": 0.02 * jax.random.normal(ks[11], (DEPTH, CMP_LEN, HEAD_DIM), f32),
        "cmp_w1_k": nrm(ks[12], (DEPTH, CMP_LEN * HEAD_DIM, CMP_HIDDEN), CMP_LEN * HEAD_DIM),
        "cmp_w2_k": nrm(ks[13], (DEPTH, CMP_HIDDEN, HEAD_DIM), CMP_HIDDEN),
        "cmp_pos_v": 0.02 * jax.random.normal(ks[14], (DEPTH, CMP_LEN, HEAD_DIM), f32),
        "cmp_w1_v": nrm(ks[15], (DEPTH, CMP_LEN * HEAD_DIM, CMP_HIDDEN), CMP_LEN * HEAD_DIM),
        "cmp_w2_v": nrm(ks[16], (DEPTH, CMP_HIDDEN, HEAD_DIM), CMP_HIDDEN),
        "w_up": nrm(ks[17], (DEPTH, D_MODEL, 2 * D_FF), D_MODEL),
        "conv_w": nrm(ks[18], (DEPTH, CONV_WIDTH, 2 * D_FF), CONV_WIDTH),
        "conv_b": 0.01 * jax.random.normal(ks[19], (DEPTH, 2 * D_FF), f32),
        "w_down": nrm(ks[20], (DEPTH, D_FF, D_MODEL), D_FF),
    }


def reference(x, c, positions, w_ada, b_ada, norm_pre_mix, norm_post_mix, norm_pre_ffn,
              norm_post_ffn, w_in, w_out, cmp_pos_k, cmp_w1_k, cmp_w2_k, cmp_pos_v,
              cmp_w1_v, cmp_w2_v, w_up, conv_w, conv_b, w_down):
    for l in range(DEPTH):
        mod = jnp.einsum('bd,de->be', jax.nn.silu(c), w_ada[l]) + b_ada[l]
        sh_a, sc_a, g_a, sh_f, sc_f, g_f = jnp.split(mod, 6, axis=-1)
        h = rmsnorm(x, norm_pre_mix[l]) * (1.0 + sc_a[:, None]) + sh_a[:, None]
        y = hybrid_mixer(h, positions, w_in[l], w_out[l], cmp_pos_k[l], cmp_w1_k[l], cmp_w2_k[l],
                         cmp_pos_v[l], cmp_w1_v[l], cmp_w2_v[l])
        x = x + g_a[:, None] * rmsnorm(y, norm_post_mix[l])
        h = rmsnorm(x, norm_pre_ffn[l]) * (1.0 + sc_f[:, None]) + sh_f[:, None]
        y = conv_ffn(h, w_up[l], conv_w[l], conv_b[l], w_down[l])
        x = x + g_f[:, None] * rmsnorm(y, norm_post_ffn[l])
    return x
```

```python
import functools

import jax
import jax.numpy as jnp
from jax import lax
from jax.experimental import pallas as pl
from jax.experimental.pallas import tpu as pltpu

F32 = jnp.float32
BF16 = jnp.bfloat16

D_MODEL = 2048
HEAD_DIM = 128
HALF = HEAD_DIM // 2
MOBA_HEADS = 8
NSA_HEADS = 8
NSA_KV_HEADS = 2
NSA_GROUP = NSA_HEADS // NSA_KV_HEADS
MOBA_BLOCK = 256
MOBA_TOPK = 3
CMP_LEN = 32
CMP_STRIDE = 16
CMP_HIDDEN = 2 * HEAD_DIM
SLC_BLOCK = 64
SLC_TOPK = 16
WINDOW = 512
D_FF = 4 * D_MODEL
ROPE_THETA = 10000.0
EPS = 1e-6
NEG = -1e30
BIG = 1e9
SCALE = HEAD_DIM ** -0.5

MOBA_WIDTH = MOBA_HEADS * HEAD_DIM
NSA_WIDTH = NSA_HEADS * HEAD_DIM
KV_WIDTH = NSA_KV_HEADS * HEAD_DIM

LANES = 128
MXU_DIM = 256
VMEM_LIMIT = 56 * 1024 * 1024

PROJ_TN = 1024
ROPE_COLS = 4096
PLAIN_COLS = 2048
PROJ_COLS = ROPE_COLS + PLAIN_COLS
N_ROPE_TILES = ROPE_COLS // PROJ_TN
N_PROJ_TILES = PROJ_COLS // PROJ_TN
CB_MQ, CB_MK, CB_NQ, CB_KC, CB_KS, CB_KW = 0, 8, 16, 24, 26, 28
CB_MV, CB_VC, CB_VS, CB_VW = 32, 40, 42, 44
GATE_COLS = 2 * LANES


def _cparams(sem):
    return pltpu.CompilerParams(dimension_semantics=sem, vmem_limit_bytes=VMEM_LIMIT)


def _dot_nt(a, b):
    return lax.dot_general(a, b, (((1,), (1,)), ((), ())), preferred_element_type=F32)


def _rms(x, w):
    ms = jnp.mean(x * x, axis=-1, keepdims=True)
    return (x * lax.rsqrt(ms + EPS)) * w


ADA_TN = 2048
ADA_TK = 512


def _ada_kernel(ct_ref, w_ref, b_ref, o_ref):
    k = pl.program_id(1)

    @pl.when(k == 0)
    def _():
        o_ref[...] = jnp.broadcast_to(b_ref[...], o_ref.shape)

    ct = ct_ref[...]
    s = ct * jax.nn.sigmoid(ct)
    w = w_ref[...]
    rows = [jnp.sum(w * s[:, b:b + 1], axis=0, keepdims=True) for b in range(ct.shape[1])]
    o_ref[...] += jnp.concatenate(rows, axis=0)


def _ada(c, w_ada, b_ada):
    bsz = c.shape[0]
    n = w_ada.shape[1]
    return pl.pallas_call(
        _ada_kernel,
        out_shape=jax.ShapeDtypeStruct((bsz, n), F32),
        grid=(n // ADA_TN, D_MODEL // ADA_TK),
        in_specs=[
            pl.BlockSpec((ADA_TK, bsz), lambda j, k: (k, 0)),
            pl.BlockSpec((ADA_TK, ADA_TN), lambda j, k: (k, j)),
            pl.BlockSpec((1, ADA_TN), lambda j, k: (0, j)),
        ],
        out_specs=pl.BlockSpec((bsz, ADA_TN), lambda j, k: (0, j)),
        compiler_params=_cparams(("arbitrary", "arbitrary")),
        name="ada",
    )(c.T, w_ada, b_ada.reshape(1, n))


ROPE_TS = 1024


def _rope_kernel(pos_ref, inv_ref, cos_ref, sin_ref):
    ang = pos_ref[...].astype(F32) * inv_ref[...]
    lane = lax.broadcasted_iota(jnp.int32, ang.shape, 1)
    cos_ref[...] = jnp.cos(ang)
    sn = jnp.sin(ang)
    sin_ref[...] = jnp.where(lane < HALF, -sn, sn)


def _rope_tables(positions):
    rows = positions.size
    inv = ROPE_THETA ** (-jnp.arange(HALF, dtype=F32) / HALF)
    inv = jnp.concatenate([inv, inv]).reshape(1, HEAD_DIM)
    return pl.pallas_call(
        _rope_kernel,
        out_shape=(jax.ShapeDtypeStruct((rows, HEAD_DIM), F32),) * 2,
        grid=(rows // ROPE_TS,),
        in_specs=[
            pl.BlockSpec((ROPE_TS, 1), lambda i: (i, 0)),
            pl.BlockSpec((1, HEAD_DIM), lambda i: (0, 0)),
        ],
        out_specs=(pl.BlockSpec((ROPE_TS, HEAD_DIM), lambda i: (i, 0)),) * 2,
        compiler_params=_cparams(("arbitrary",)),
        name="rope_tables",
    )(positions.reshape(rows, 1), inv)


PROJ_TM = 512


def _inproj_kernel(x_ref, nw_ref, sh_ref, sc_ref, cos_ref, sin_ref, w_ref, o_ref, g_ref, h_scr):
    j = pl.program_id(1)

    @pl.when(j == 0)
    def _():
        h = _rms(x_ref[...], nw_ref[...]) * (1.0 + sc_ref[0]) + sh_ref[0]
        h_scr[...] = h.astype(BF16)

    acc = jnp.dot(h_scr[...], w_ref[...], preferred_element_type=F32)

    @pl.when(j < N_ROPE_TILES)
    def _():
        cos = cos_ref[...]
        sin = sin_ref[...]
        for g in range(PROJ_TN // HEAD_DIM):
            y = acc[:, g * HEAD_DIM:(g + 1) * HEAD_DIM]
            o_ref[:, g * HEAD_DIM:(g + 1) * HEAD_DIM] = (
                y * cos + pltpu.roll(y, HALF, 1) * sin).astype(BF16)

    @pl.when(j >= N_ROPE_TILES)
    def _():
        o_ref[...] = acc.astype(BF16)

    @pl.when(j == N_PROJ_TILES - 1)
    def _():
        g_ref[...] = acc[:, PROJ_TN - GATE_COLS:]


def _inproj(x2d, nw, mod3, cos, sin, w_proj, seq):
    rows = x2d.shape[0]
    tpb = seq // PROJ_TM
    return pl.pallas_call(
        _inproj_kernel,
        out_shape=(jax.ShapeDtypeStruct((rows, PROJ_COLS), BF16),
                   jax.ShapeDtypeStruct((rows, GATE_COLS), F32)),
        grid=(rows // PROJ_TM, N_PROJ_TILES),
        in_specs=[
            pl.BlockSpec((PROJ_TM, D_MODEL), lambda i, j: (i, 0)),
            pl.BlockSpec((1, D_MODEL), lambda i, j: (0, 0)),
            pl.BlockSpec((1, 1, D_MODEL), lambda i, j: ((i // tpb) * 6 + 0, 0, 0)),
            pl.BlockSpec((1, 1, D_MODEL), lambda i, j: ((i // tpb) * 6 + 1, 0, 0)),
            pl.BlockSpec((PROJ_TM, HEAD_DIM), lambda i, j: (i, 0)),
            pl.BlockSpec((PROJ_TM, HEAD_DIM), lambda i, j: (i, 0)),
            pl.BlockSpec((D_MODEL, PROJ_TN), lambda i, j: (0, j)),
        ],
        out_specs=(pl.BlockSpec((PROJ_TM, PROJ_TN), lambda i, j: (i, j)),
                   pl.BlockSpec((PROJ_TM, GATE_COLS), lambda i, j: (i, 0))),
        scratch_shapes=[pltpu.VMEM((PROJ_TM, D_MODEL), BF16)],
        compiler_params=_cparams(("arbitrary", "arbitrary")),
        name="inproj",
    )(x2d, nw, mod3, mod3, cos, sin, w_proj)


def _flash_init(m_scr, l_scr, acc_scr):
    m_scr[...] = jnp.full(m_scr.shape, NEG, F32)
    l_scr[...] = jnp.zeros(l_scr.shape, F32)
    acc_scr[...] = jnp.zeros(acc_scr.shape, F32)


def _flash_update(s, v, m_scr, l_scr, acc_scr):
    m_prev = m_scr[...]
    m_new = jnp.maximum(m_prev, jnp.max(s, axis=-1, keepdims=True))
    alpha = jnp.exp(m_prev - m_new)
    p = jnp.exp(s - m_new)
    l_scr[...] = alpha * l_scr[...] + jnp.sum(p, axis=-1, keepdims=True)
    acc_scr[...] = alpha * acc_scr[...] + jnp.dot(p.astype(BF16), v, preferred_element_type=F32)
    m_scr[...] = m_new


def _rank_desc(score, axis, count):
    idx = lax.broadcasted_iota(jnp.int32, score.shape, axis)
    rank = jnp.zeros(score.shape, F32)
    for jp in range(count):
        other = score[:, jp:jp + 1] if axis == 1 else score[jp:jp + 1, :]
        beats = (other > score) | ((other == score) & (idx > jp))
        rank = rank + jnp.where(beats, 1.0, 0.0)
    return rank


MOBA_TQ = 512
MOBA_DIAG = MOBA_TQ // MOBA_BLOCK


def _moba_kernel(q_ref, k_ref, v_ref, o_ref, kmean_scr, m_scr, l_scr, acc_scr, *, nblk):
    qi = pl.program_id(2)

    @pl.when(qi == 0)
    def _():
        kmean_scr[...] = jnp.zeros(kmean_scr.shape, F32)
        for n in range(nblk):
            kb = k_ref[n * MOBA_BLOCK:(n + 1) * MOBA_BLOCK, :].astype(F32)
            kmean_scr[n:n + 1, :] = jnp.mean(kb, axis=0, keepdims=True)

    q = q_ref[...]
    gate = _dot_nt(q, kmean_scr[...].astype(BF16))
    row = lax.broadcasted_iota(jnp.int32, gate.shape, 0)
    col = lax.broadcasted_iota(jnp.int32, gate.shape, 1)
    qblk = jnp.right_shift(qi * MOBA_TQ + row, MOBA_BLOCK.bit_length() - 1)
    past = col < qblk
    gate = jnp.where(past, gate, NEG)
    rank = _rank_desc(gate, 1, nblk)
    allow = (past & (rank < float(MOBA_TOPK))) | (col == qblk)
    bias = jnp.where(allow, 0.0, NEG).astype(BF16)
    q_aug = jnp.concatenate([q, bias], axis=1)

    _flash_init(m_scr, l_scr, acc_scr)
    kcol = lax.broadcasted_iota(jnp.int32, (MOBA_BLOCK, LANES), 1)

    def scores(j):
        start = pl.multiple_of(j * MOBA_BLOCK, MOBA_BLOCK)
        kt = k_ref[pl.ds(start, MOBA_BLOCK), :]
        vt = v_ref[pl.ds(start, MOBA_BLOCK), :]
        onehot = jnp.where(kcol == j, 1.0, 0.0).astype(BF16)
        k_aug = jnp.concatenate([kt, onehot], axis=1)
        return _dot_nt(q_aug, k_aug) * SCALE, vt

    qpos = qi * MOBA_TQ + lax.broadcasted_iota(jnp.int32, (MOBA_TQ, MOBA_BLOCK), 0)
    kofs = lax.broadcasted_iota(jnp.int32, (MOBA_TQ, MOBA_BLOCK), 1)
    for d in range(MOBA_DIAG):
        j = qi * MOBA_DIAG + d
        s, vt = scores(j)
        s = jnp.where(j * MOBA_BLOCK + kofs <= qpos, s, NEG)
        _flash_update(s, vt, m_scr, l_scr, acc_scr)

    def body(j, carry):
        s, vt = scores(j)
        _flash_update(s, vt, m_scr, l_scr, acc_scr)
        return carry

    lax.fori_loop(0, qi * MOBA_DIAG, body, 0)
    o_ref[...] = (acc_scr[...] / l_scr[...]).astype(BF16)


def _moba(proj, bsz, seq):
    nq = seq // MOBA_TQ
    nblk = seq // MOBA_BLOCK
    return pl.pallas_call(
        functools.partial(_moba_kernel, nblk=nblk),
        out_shape=jax.ShapeDtypeStruct((bsz * seq, MOBA_WIDTH), BF16),
        grid=(bsz, MOBA_HEADS, nq),
        in_specs=[
            pl.BlockSpec((MOBA_TQ, HEAD_DIM), lambda b, h, i: (b * nq + i, CB_MQ + h)),
            pl.BlockSpec((seq, HEAD_DIM), lambda b, h, i: (b, CB_MK + h)),
            pl.BlockSpec((seq, HEAD_DIM), lambda b, h, i: (b, CB_MV + h)),
        ],
        out_specs=pl.BlockSpec((MOBA_TQ, HEAD_DIM), lambda b, h, i: (b * nq + i, h)),
        scratch_shapes=[
            pltpu.VMEM((LANES, HEAD_DIM), F32),
            pltpu.VMEM((MOBA_TQ, 1), F32),
            pltpu.VMEM((MOBA_TQ, 1), F32),
            pltpu.VMEM((MOBA_TQ, HEAD_DIM), F32),
        ],
        compiler_params=_cparams(("arbitrary", "arbitrary", "arbitrary")),
        name="moba",
    )(proj, proj, proj)


CMP_HALF_FEAT = CMP_STRIDE * HEAD_DIM


def _compress_kernel(x_ref, pos_ref, w1f_ref, w1_ref, w2_ref, o_ref):
    x = x_ref[0, 0]
    a = jnp.dot(x, w1_ref[:CMP_HALF_FEAT, :], preferred_element_type=F32)
    b = jnp.dot(x, w1_ref[CMP_HALF_FEAT:, :], preferred_element_type=F32)
    nch = x.shape[0]
    b_next = pltpu.roll(b, nch - 1, 0)
    pos_bias = jnp.sum(w1f_ref[...] * pos_ref[...], axis=0, keepdims=True)
    hid = jax.nn.gelu(a + b_next + pos_bias, approximate=True)
    o_ref[0, 0] = jnp.dot(hid.astype(BF16), w2_ref[...], preferred_element_type=F32).astype(BF16)


def _compress(xc, pos_col, w1, w2):
    bsz, hk, nch, feat = xc.shape
    return pl.pallas_call(
        _compress_kernel,
        out_shape=jax.ShapeDtypeStruct((bsz, hk, nch, HEAD_DIM), BF16),
        grid=(bsz, hk),
        in_specs=[
            pl.BlockSpec((1, 1, nch, feat), lambda b, k: (b, k, 0, 0)),
            pl.BlockSpec((CMP_LEN * HEAD_DIM, 1), lambda b, k: (0, 0)),
            pl.BlockSpec((CMP_LEN * HEAD_DIM, CMP_HIDDEN), lambda b, k: (0, 0)),
            pl.BlockSpec((CMP_LEN * HEAD_DIM, CMP_HIDDEN), lambda b, k: (0, 0)),
            pl.BlockSpec((CMP_HIDDEN, HEAD_DIM), lambda b, k: (0, 0)),
        ],
        out_specs=pl.BlockSpec((1, 1, nch, HEAD_DIM), lambda b, k: (b, k, 0, 0)),
        compiler_params=_cparams(("arbitrary", "arbitrary")),
        name="compress",
    )(xc, pos_col, w1, w1.astype(BF16), w2.astype(BF16))


NSA_TQ = 256
NSA_ROWS = NSA_GROUP * NSA_TQ


def _split3(x):
    hi = x.astype(BF16)
    r1 = x - hi.astype(F32)
    mid = r1.astype(BF16)
    lo = (r1 - mid.astype(F32)).astype(BF16)
    return hi, mid, lo


def _nsa_kernel(q_ref, kc_ref, vc_ref, ks_ref, vs_ref, kw_ref, vw_ref, gl_ref, mt_ref, o_ref,
                m_scr, l_scr, acc_scr, osl_scr):
    qi = pl.program_id(2)
    tq = NSA_TQ
    q = q_ref[...]
    qg = jnp.concatenate([q[:, g * HEAD_DIM:(g + 1) * HEAD_DIM] for g in range(NSA_GROUP)], axis=0)
    ncmp = kc_ref.shape[2]

    rowc = lax.broadcasted_iota(jnp.int32, (NSA_ROWS, ncmp), 0)
    colc = lax.broadcasted_iota(jnp.int32, (NSA_ROWS, ncmp), 1)
    tpos = qi * tq + jnp.bitwise_and(rowc, tq - 1)
    cmask = colc * CMP_STRIDE + (CMP_LEN - 1) <= tpos
    s = jnp.where(cmask, _dot_nt(qg, kc_ref[0, 0]) * SCALE, NEG)
    e = jnp.exp(s - jnp.max(s, axis=-1, keepdims=True))
    p = jnp.where(cmask, e / jnp.sum(e, axis=-1, keepdims=True), 0.0)
    o_cmp = jnp.dot(p.astype(BF16), vc_ref[0, 0], preferred_element_type=F32)

    imp = p[0:tq]
    for g in range(1, NSA_GROUP):
        imp = imp + p[g * tq:(g + 1) * tq]
    mt = mt_ref[...]
    p_slc = sum(_dot_nt(mt, part) for part in _split3(imp))
    jrow = lax.broadcasted_iota(jnp.int32, p_slc.shape, 0)
    tq_l = qi * tq + lax.broadcasted_iota(jnp.int32, p_slc.shape, 1)
    jt = jnp.right_shift(tq_l, SLC_BLOCK.bit_length() - 1)
    valid = jrow <= jt
    forced = (jrow == 0) | (jrow == jt) | (jrow == jt - 1)
    score = jnp.where(valid & forced, BIG, jnp.where(valid, p_slc, NEG))
    n_slc = ks_ref.shape[0] // SLC_BLOCK
    rank = _rank_desc(score, 0, n_slc)
    sel = valid & (rank < float(min(SLC_TOPK, n_slc)))
    bias = jnp.where(sel, 0.0, NEG).T.astype(BF16)
    q_aug = jnp.concatenate([qg, jnp.concatenate([bias] * NSA_GROUP, axis=0)], axis=1)

    qpos = qi * tq + jnp.bitwise_and(lax.broadcasted_iota(jnp.int32, (NSA_ROWS, tq), 0), tq - 1)
    kofs = lax.broadcasted_iota(jnp.int32, (NSA_ROWS, tq), 1)

    _flash_init(m_scr, l_scr, acc_scr)
    krow_blk = jnp.right_shift(lax.broadcasted_iota(jnp.int32, (tq, LANES), 0), SLC_BLOCK.bit_length() - 1)
    kcol = lax.broadcasted_iota(jnp.int32, (tq, LANES), 1)

    def slc_scores(j):
        start = pl.multiple_of(j * tq, tq)
        kt = ks_ref[pl.ds(start, tq), :]
        vt = vs_ref[pl.ds(start, tq), :]
        onehot = jnp.where(kcol == j * (tq // SLC_BLOCK) + krow_blk, 1.0, 0.0).astype(BF16)
        return _dot_nt(q_aug, jnp.concatenate([kt, onehot], axis=1)) * SCALE, vt

    s, vt = slc_scores(qi)
    _flash_update(jnp.where(qi * tq + kofs <= qpos, s, NEG), vt, m_scr, l_scr, acc_scr)

    def slc_body(j, carry):
        s, vt = slc_scores(j)
        _flash_update(s, vt, m_scr, l_scr, acc_scr)
        return carry

    lax.fori_loop(0, qi, slc_body, 0)
    osl_scr[...] = acc_scr[...] / l_scr[...]

    _flash_init(m_scr, l_scr, acc_scr)

    def win_step(j):
        start = pl.multiple_of(j * tq, tq)
        kt = kw_ref[pl.ds(start, tq), :]
        vt = vw_ref[pl.ds(start, tq), :]
        kpos = j * tq + kofs
        s = _dot_nt(qg, kt) * SCALE
        s = jnp.where((kpos <= qpos) & (kpos > qpos - WINDOW), s, NEG)
        _flash_update(s, vt, m_scr, l_scr, acc_scr)

    win_step(qi)
    for back in range(1, WINDOW // tq + 1):
        @pl.when(qi >= back)
        def _(back=back):
            win_step(qi - back)
    o_win = acc_scr[...] / l_scr[...]
    o_slc = osl_scr[...]

    sig = jax.nn.sigmoid(gl_ref[...])
    for g in range(NSA_GROUP):
        rows = slice(g * tq, (g + 1) * tq)
        og = (sig[:, g:g + 1] * o_cmp[rows]
              + sig[:, NSA_GROUP + g:NSA_GROUP + g + 1] * o_slc[rows]
              + sig[:, 2 * NSA_GROUP + g:2 * NSA_GROUP + g + 1] * o_win[rows])
        o_ref[:, g * HEAD_DIM:(g + 1) * HEAD_DIM] = og.astype(BF16)


def _slc_weight_matrix(ncmp):
    rs, rc = SLC_BLOCK // CMP_STRIDE, CMP_LEN // CMP_STRIDE
    j = jnp.arange(LANES)[:, None]
    n = jnp.arange(ncmp)[None, :]
    w = jnp.zeros((LANES, ncmp), F32)
    for o in range(rs + rc - 1):
        w_o = float(sum(1 for m in range(rs) for nn in range(rc) if m - nn + rc - 1 == o))
        w = w + jnp.where(n == rs * j + o - (rc - 1), w_o, 0.0)
    return w.astype(BF16)


def _nsa(proj, gates, kc, vc, bsz, seq):
    nq = seq // NSA_TQ
    ncmp = kc.shape[2]
    qw = NSA_GROUP * HEAD_DIM
    kv_spec = lambda cb: pl.BlockSpec((seq, HEAD_DIM), lambda b, k, i: (b, cb + k))
    cmp_spec = pl.BlockSpec((1, 1, ncmp, HEAD_DIM), lambda b, k, i: (b, k, 0, 0))
    return pl.pallas_call(
        _nsa_kernel,
        out_shape=jax.ShapeDtypeStruct((bsz * seq, NSA_WIDTH), BF16),
        grid=(bsz, NSA_KV_HEADS, nq),
        in_specs=[
            pl.BlockSpec((NSA_TQ, qw), lambda b, k, i: (b * nq + i, CB_NQ * HEAD_DIM // qw + k)),
            cmp_spec, cmp_spec,
            kv_spec(CB_KS), kv_spec(CB_VS), kv_spec(CB_KW), kv_spec(CB_VW),
            pl.BlockSpec((NSA_TQ, LANES), lambda b, k, i: (b * nq + i, k)),
            pl.BlockSpec((LANES, ncmp), lambda b, k, i: (0, 0)),
        ],
        out_specs=pl.BlockSpec((NSA_TQ, qw), lambda b, k, i: (b * nq + i, k)),
        scratch_shapes=[
            pltpu.VMEM((NSA_ROWS, 1), F32),
            pltpu.VMEM((NSA_ROWS, 1), F32),
            pltpu.VMEM((NSA_ROWS, HEAD_DIM), F32),
            pltpu.VMEM((NSA_ROWS, HEAD_DIM), F32),
        ],
        compiler_params=_cparams(("arbitrary", "arbitrary", "arbitrary")),
        name="nsa",
    )(proj, kc, vc, proj, proj, proj, proj, gates, _slc_weight_matrix(ncmp))


OUT_TM = 512


def _outproj_kernel(om_ref, on_ref, wm_ref, wn_ref, x_ref, ga_ref, npost_ref, npre_ref, sh_ref, sc_ref,
                    x1_ref, h2_ref):
    y = (jnp.dot(om_ref[...], wm_ref[...], preferred_element_type=F32)
         + jnp.dot(on_ref[...], wn_ref[...], preferred_element_type=F32))
    x1 = x_ref[...] + ga_ref[0] * _rms(y, npost_ref[...])
    x1_ref[...] = x1
    h2_ref[...] = (_rms(x1, npre_ref[...]) * (1.0 + sc_ref[0]) + sh_ref[0]).astype(BF16)


def _outproj(o_moba, o_nsa, w_out, x2d, mod3, npost, npre, seq):
    rows = x2d.shape[0]
    tpb = seq // OUT_TM
    mod_spec = lambda idx: pl.BlockSpec((1, 1, D_MODEL), lambda i: ((i // tpb) * 6 + idx, 0, 0))
    vec_spec = pl.BlockSpec((1, D_MODEL), lambda i: (0, 0))
    w_bf = w_out.astype(BF16)
    return pl.pallas_call(
        _outproj_kernel,
        out_shape=(jax.ShapeDtypeStruct((rows, D_MODEL), F32),
                   jax.ShapeDtypeStruct((rows, D_MODEL), BF16)),
        grid=(rows // OUT_TM,),
        in_specs=[
            pl.BlockSpec((OUT_TM, MOBA_WIDTH), lambda i: (i, 0)),
            pl.BlockSpec((OUT_TM, NSA_WIDTH), lambda i: (i, 0)),
            pl.BlockSpec((MOBA_WIDTH, D_MODEL), lambda i: (0, 0)),
            pl.BlockSpec((NSA_WIDTH, D_MODEL), lambda i: (0, 0)),
            pl.BlockSpec((OUT_TM, D_MODEL), lambda i: (i, 0)),
            mod_spec(2), vec_spec, vec_spec, mod_spec(3), mod_spec(4),
        ],
        out_specs=(pl.BlockSpec((OUT_TM, D_MODEL), lambda i: (i, 0)),
                   pl.BlockSpec((OUT_TM, D_MODEL), lambda i: (i, 0))),
        compiler_params=_cparams(("arbitrary",)),
        name="outproj",
    )(o_moba, o_nsa, w_bf[:MOBA_WIDTH], w_bf[MOBA_WIDTH:], x2d, mod3, npost, npre, mod3, mod3)


FFN_TM = 512
FFN_TF = 512
FFN_HALO = 16
CONV_WIDTH = 3


def _ffn_kernel(h_ref, halo_ref, wg_ref, wv_ref, cwg_ref, cwv_ref, cbg_ref, cbv_ref, wd_ref,
                x1_ref, gf_ref, npost_ref, o_ref, acc_scr, u_scr, *, tiles_per_seq):
    i = pl.program_id(0)
    j = pl.program_id(1)
    halo = halo_ref[...]
    halo = jnp.where(i % tiles_per_seq == 0, jnp.zeros_like(halo), halo)
    h_aug = jnp.concatenate([halo, h_ref[...]], axis=0)

    def conv_branch(w_ref, cw_ref, cb_ref):
        u_scr[...] = jnp.dot(h_aug, w_ref[...], preferred_element_type=F32)
        cw = cw_ref[...]
        out = cb_ref[...]
        for tap in range(CONV_WIDTH):
            ofs = FFN_HALO - (CONV_WIDTH - 1) + tap
            out = out + cw[tap:tap + 1, :] * u_scr[pl.ds(ofs, FFN_TM), :]
        return out

    gate = conv_branch(wg_ref, cwg_ref, cbg_ref)
    val = conv_branch(wv_ref, cwv_ref, cbv_ref)
    act = (jax.nn.gelu(gate, approximate=True) * val).astype(BF16)
    contrib = jnp.dot(act, wd_ref[...], preferred_element_type=F32)

    @pl.when(j == 0)
    def _():
        acc_scr[...] = contrib

    @pl.when(j > 0)
    def _():
        acc_scr[...] += contrib

    @pl.when(j == pl.num_programs(1) - 1)
    def _():
        o_ref[...] = x1_ref[...] + gf_ref[0] * _rms(acc_scr[...], npost_ref[...])


def _ffn(h2, x1, w_up, conv_w, conv_b, w_down, mod3, npost, seq):
    rows = h2.shape[0]
    tps = seq // FFN_TM
    nj = D_FF // FFN_TF
    w_up_bf = w_up.astype(BF16)
    cb = conv_b.reshape(1, 2 * D_FF)
    return pl.pallas_call(
        functools.partial(_ffn_kernel, tiles_per_seq=tps),
        out_shape=jax.ShapeDtypeStruct((rows, D_MODEL), F32),
        grid=(rows // FFN_TM, nj),
        in_specs=[
            pl.BlockSpec((FFN_TM, D_MODEL), lambda i, j: (i, 0)),
            pl.BlockSpec((FFN_HALO, D_MODEL),
                         lambda i, j: (jnp.maximum(i * (FFN_TM // FFN_HALO) - 1, 0), 0)),
            pl.BlockSpec((D_MODEL, FFN_TF), lambda i, j: (0, j)),
            pl.BlockSpec((D_MODEL, FFN_TF), lambda i, j: (0, nj + j)),
            pl.BlockSpec((CONV_WIDTH, FFN_TF), lambda i, j: (0, j)),
            pl.BlockSpec((CONV_WIDTH, FFN_TF), lambda i, j: (0, nj + j)),
            pl.BlockSpec((1, FFN_TF), lambda i, j: (0, j)),
            pl.BlockSpec((1, FFN_TF), lambda i, j: (0, nj + j)),
            pl.BlockSpec((FFN_TF, D_MODEL), lambda i, j: (j, 0)),
            pl.BlockSpec((FFN_TM, D_MODEL), lambda i, j: (i, 0)),
            pl.BlockSpec((1, 1, D_MODEL), lambda i, j: ((i // tps) * 6 + 5, 0, 0)),
            pl.BlockSpec((1, D_MODEL), lambda i, j: (0, 0)),
        ],
        out_specs=pl.BlockSpec((FFN_TM, D_MODEL), lambda i, j: (i, 0)),
        scratch_shapes=[
            pltpu.VMEM((FFN_TM, D_MODEL), F32),
            pltpu.VMEM((FFN_HALO + FFN_TM, FFN_TF), F32),
        ],
        compiler_params=_cparams(("arbitrary", "arbitrary")),
        name="ffn",
    )(h2, h2, w_up_bf, w_up_bf, conv_w, conv_w, cb, cb, w_down.astype(BF16), x1, mod3, npost)


def _reorder_w_in(w_in):
    c = 0
    parts = {}
    for name, width in (("mq", MOBA_WIDTH), ("mk", MOBA_WIDTH), ("mv", MOBA_WIDTH), ("nq", NSA_WIDTH),
                        ("kc", KV_WIDTH), ("vc", KV_WIDTH), ("ks", KV_WIDTH), ("vs", KV_WIDTH),
                        ("kw", KV_WIDTH), ("vw", KV_WIDTH), ("ng", 3 * NSA_HEADS)):
        parts[name] = w_in[:, c:c + width]
        c += width
    ng = parts["ng"].reshape(D_MODEL, 3, NSA_KV_HEADS, NSA_GROUP).transpose(0, 2, 1, 3)
    ng = ng.reshape(D_MODEL, NSA_KV_HEADS, 3 * NSA_GROUP)
    ng = jnp.pad(ng, ((0, 0), (0, 0), (0, LANES - 3 * NSA_GROUP))).reshape(D_MODEL, GATE_COLS)
    rope = [parts[n] for n in ("mq", "mk", "nq", "kc", "ks", "kw")]
    rope_w = sum(p.shape[1] for p in rope)
    rope.append(jnp.zeros((D_MODEL, ROPE_COLS - rope_w), w_in.dtype))
    plain = [parts[n] for n in ("mv", "vc", "vs", "vw")] + [ng]
    return jnp.concatenate(rope + plain, axis=1).astype(BF16)


def _chunks(proj, cb, bsz, seq):
    a = proj[:, cb * HEAD_DIM:(cb + NSA_KV_HEADS) * HEAD_DIM]
    a = a.reshape(bsz, seq // CMP_STRIDE, CMP_STRIDE, NSA_KV_HEADS, HEAD_DIM).transpose(0, 3, 1, 2, 4)
    return a.reshape(bsz, NSA_KV_HEADS, seq // CMP_STRIDE, CMP_HALF_FEAT)


def kernel(x, c, positions, w_ada, b_ada, norm_pre_mix, norm_post_mix, norm_pre_ffn, norm_post_ffn,
           w_in, w_out, cmp_pos_k, cmp_w1_k, cmp_w2_k, cmp_pos_v, cmp_w1_v, cmp_w2_v,
           w_up, conv_w, conv_b, w_down):
    bsz, seq, d = x.shape
    depth = w_in.shape[0]
    assert d == D_MODEL and seq % WINDOW == 0 and seq % PROJ_TM == 0
    cos, sin = _rope_tables(positions)
    x2d = x.reshape(bsz * seq, d)
    for l in range(depth):
        mod3 = _ada(c, w_ada[l], b_ada[l]).reshape(bsz * 6, 1, d)
        proj, gates = _inproj(x2d, norm_pre_mix[l].reshape(1, d), mod3, cos, sin,
                              _reorder_w_in(w_in[l]), seq)
        o_moba = _moba(proj, bsz, seq)
        kc = _compress(_chunks(proj, CB_KC, bsz, seq), cmp_pos_k[l].reshape(-1, 1), cmp_w1_k[l], cmp_w2_k[l])
        vc = _compress(_chunks(proj, CB_VC, bsz, seq), cmp_pos_v[l].reshape(-1, 1), cmp_w1_v[l], cmp_w2_v[l])
        o_nsa = _nsa(proj, gates, kc, vc, bsz, seq)
        x1, h2 = _outproj(o_moba, o_nsa, w_out[l], x2d, mod3, norm_post_mix[l].reshape(1, d),
                          norm_pre_ffn[l].reshape(1, d), seq)
        x2d = _ffn(h2, x1, w_up[l], conv_w[l], conv_b[l], w_down[l], mod3,
                   norm_post_ffn[l].reshape(1, d), seq)
    return x2d.reshape(bsz, seq, d)
```

```python
import functools

import jax
import jax.numpy as jnp
from jax import lax
from jax.experimental import pallas as pl
from jax.experimental.pallas import tpu as pltpu

F32 = jnp.float32
BF16 = jnp.bfloat16

D_MODEL = 2048
HEAD_DIM = 128
HALF = HEAD_DIM // 2
MOBA_HEADS = 8
NSA_HEADS = 8
NSA_KV_HEADS = 2
NSA_GROUP = NSA_HEADS // NSA_KV_HEADS
MOBA_BLOCK = 256
MOBA_TOPK = 3
CMP_LEN = 32
CMP_STRIDE = 16
CMP_HIDDEN = 2 * HEAD_DIM
SLC_BLOCK = 64
SLC_TOPK = 16
WINDOW = 512
D_FF = 4 * D_MODEL
ROPE_THETA = 10000.0
EPS = 1e-6
NEG = -1e30
BIG = 1e9
SCALE = HEAD_DIM ** -0.5

MOBA_WIDTH = MOBA_HEADS * HEAD_DIM
NSA_WIDTH = NSA_HEADS * HEAD_DIM
KV_WIDTH = NSA_KV_HEADS * HEAD_DIM

LANES = 128
MXU_DIM = 256
VMEM_LIMIT = 56 * 1024 * 1024

PROJ_TN = 1024
ROPE_COLS = 4096
PLAIN_COLS = 2048
PROJ_COLS = ROPE_COLS + PLAIN_COLS
N_ROPE_TILES = ROPE_COLS // PROJ_TN
N_PROJ_TILES = PROJ_COLS // PROJ_TN
CB_MQ, CB_MK, CB_NQ, CB_KC, CB_KS, CB_KW = 0, 8, 16, 24, 26, 28
CB_MV, CB_VC, CB_VS, CB_VW = 32, 40, 42, 44
GATE_COLS = 2 * LANES


def _cparams(sem):
    return pltpu.CompilerParams(dimension_semantics=sem, vmem_limit_bytes=VMEM_LIMIT)


def _dot_nt(a, b):
    return lax.dot_general(a, b, (((1,), (1,)), ((), ())), preferred_element_type=F32)


def _rms(x, w):
    ms = jnp.mean(x * x, axis=-1, keepdims=True)
    return (x * lax.rsqrt(ms + EPS)) * w


ADA_TN = 2048
ADA_TK = 512


def _ada_kernel(ct_ref, w_ref, b_ref, o_ref):
    k = pl.program_id(1)

    @pl.when(k == 0)
    def _():
        o_ref[...] = jnp.broadcast_to(b_ref[...], o_ref.shape)

    ct = ct_ref[...]
    s = ct * jax.nn.sigmoid(ct)
    w = w_ref[...]
    rows = [jnp.sum(w * s[:, b:b + 1], axis=0, keepdims=True) for b in range(ct.shape[1])]
    o_ref[...] += jnp.concatenate(rows, axis=0)


def _ada(c, w_ada, b_ada):
    bsz = c.shape[0]
    n = w_ada.shape[1]
    return pl.pallas_call(
        _ada_kernel,
        out_shape=jax.ShapeDtypeStruct((bsz, n), F32),
        grid=(n // ADA_TN, D_MODEL // ADA_TK),
        in_specs=[
            pl.BlockSpec((ADA_TK, bsz), lambda j, k: (k, 0)),
            pl.BlockSpec((ADA_TK, ADA_TN), lambda j, k: (k, j)),
            pl.BlockSpec((1, ADA_TN), lambda j, k: (0, j)),
        ],
        out_specs=pl.BlockSpec((bsz, ADA_TN), lambda j, k: (0, j)),
        compiler_params=_cparams(("arbitrary", "arbitrary")),
        name="ada",
    )(c.T, w_ada, b_ada.reshape(1, n))


ROPE_TS = 1024


def _rope_kernel(pos_ref, inv_ref, cos_ref, sin_ref):
    ang = pos_ref[...].astype(F32) * inv_ref[...]
    lane = lax.broadcasted_iota(jnp.int32, ang.shape, 1)
    cos_ref[...] = jnp.cos(ang)
    sn = jnp.sin(ang)
    sin_ref[...] = jnp.where(lane < HALF, -sn, sn)


def _rope_tables(positions):
    rows = positions.size
    inv = ROPE_THETA ** (-jnp.arange(HALF, dtype=F32) / HALF)
    inv = jnp.concatenate([inv, inv]).reshape(1, HEAD_DIM)
    return pl.pallas_call(
        _rope_kernel,
        out_shape=(jax.ShapeDtypeStruct((rows, HEAD_DIM), F32),) * 2,
        grid=(rows // ROPE_TS,),
        in_specs=[
            pl.BlockSpec((ROPE_TS, 1), lambda i: (i, 0)),
            pl.BlockSpec((1, HEAD_DIM), lambda i: (0, 0)),
        ],
        out_specs=(pl.BlockSpec((ROPE_TS, HEAD_DIM), lambda i: (i, 0)),) * 2,
        compiler_params=_cparams(("arbitrary",)),
        name="rope_tables",
    )(positions.reshape(rows, 1), inv)


PROJ_TM = 512


def _inproj_kernel(x_ref, nw_ref, sh_ref, sc_ref, cos_ref, sin_ref, w_ref, o_ref, g_ref, h_scr):
    j = pl.program_id(1)

    @pl.when(j == 0)
    def _():
        h = _rms(x_ref[...], nw_ref[...]) * (1.0 + sc_ref[0]) + sh_ref[0]
        h_scr[...] = h.astype(BF16)

    acc = jnp.dot(h_scr[...], w_ref[...], preferred_element_type=F32)

    @pl.when(j < N_ROPE_TILES)
    def _():
        cos = cos_ref[...]
        sin = sin_ref[...]
        for g in range(PROJ_TN // HEAD_DIM):
            y = acc[:, g * HEAD_DIM:(g + 1) * HEAD_DIM]
            o_ref[:, g * HEAD_DIM:(g + 1) * HEAD_DIM] = (
                y * cos + pltpu.roll(y, HALF, 1) * sin).astype(BF16)

    @pl.when(j >= N_ROPE_TILES)
    def _():
        o_ref[...] = acc.astype(BF16)

    @pl.when(j == N_PROJ_TILES - 1)
    def _():
        g_ref[...] = acc[:, PROJ_TN - GATE_COLS:]


def _inproj(x2d, nw, mod3, cos, sin, w_proj, seq):
    rows = x2d.shape[0]
    tpb = seq // PROJ_TM
    return pl.pallas_call(
        _inproj_kernel,
        out_shape=(jax.ShapeDtypeStruct((rows, PROJ_COLS), BF16),
                   jax.ShapeDtypeStruct((rows, GATE_COLS), F32)),
        grid=(rows // PROJ_TM, N_PROJ_TILES),
        in_specs=[
            pl.BlockSpec((PROJ_TM, D_MODEL), lambda i, j: (i, 0)),
            pl.BlockSpec((1, D_MODEL), lambda i, j: (0, 0)),
            pl.BlockSpec((1, 1, D_MODEL), lambda i, j: ((i // tpb) * 6 + 0, 0, 0)),
            pl.BlockSpec((1, 1, D_MODEL), lambda i, j: ((i // tpb) * 6 + 1, 0, 0)),
            pl.BlockSpec((PROJ_TM, HEAD_DIM), lambda i, j: (i, 0)),
            pl.BlockSpec((PROJ_TM, HEAD_DIM), lambda i, j: (i, 0)),
            pl.BlockSpec((D_MODEL, PROJ_TN), lambda i, j: (0, j)),
        ],
        out_specs=(pl.BlockSpec((PROJ_TM, PROJ_TN), lambda i, j: (i, j)),
                   pl.BlockSpec((PROJ_TM, GATE_COLS), lambda i, j: (i, 0))),
        scratch_shapes=[pltpu.VMEM((PROJ_TM, D_MODEL), BF16)],
        compiler_params=_cparams(("arbitrary", "arbitrary")),
        name="inproj",
    )(x2d, nw, mod3, mod3, cos, sin, w_proj)


EXP2_SCALE = SCALE * 1.4426950408889634


def _lane_tiles(x):
    return [x[:, t * LANES:(t + 1) * LANES] for t in range(x.shape[1] // LANES)]


def _lane_fold(x, op):
    return functools.reduce(op, _lane_tiles(x))


def _softmax_pv(s, v):
    m = jnp.max(s, axis=-1, keepdims=True)
    p = jnp.exp2((s - m) * EXP2_SCALE)
    l = jnp.sum(p, axis=-1, keepdims=True)
    return jnp.dot(p.astype(BF16), v, preferred_element_type=F32) * (1.0 / l)


def _two_pass_attention(qk_fn, v_fn, last_mask_fn, n_full, s_scr, m_scr, l_scr, acc_scr):
    m_scr[...] = jnp.full(m_scr.shape, NEG, F32)

    def pass1(j, carry):
        s = qk_fn(j)
        s_scr[j] = s
        m_scr[...] = jnp.maximum(m_scr[...], _lane_fold(s, jnp.maximum))
        return carry

    lax.fori_loop(0, n_full, pass1, 0)
    s = last_mask_fn(qk_fn(n_full))
    s_scr[n_full] = s
    m = jnp.max(jnp.maximum(m_scr[...], _lane_fold(s, jnp.maximum)), axis=-1, keepdims=True)
    m_scr[...] = jnp.broadcast_to(m, m_scr.shape)
    l_scr[...] = jnp.zeros(l_scr.shape, F32)
    acc_scr[...] = jnp.zeros(acc_scr.shape, F32)

    def pass2(j, carry):
        mb = m_scr[...]
        ps = [jnp.exp2((st - mb) * EXP2_SCALE) for st in _lane_tiles(s_scr[j])]
        l_scr[...] += functools.reduce(jnp.add, ps)
        p = jnp.concatenate(ps, axis=1).astype(BF16)
        acc_scr[...] += jnp.dot(p, v_fn(j), preferred_element_type=F32)
        return carry

    lax.fori_loop(0, n_full + 1, pass2, 0)
    l = jnp.sum(l_scr[...], axis=-1, keepdims=True)
    return acc_scr[...] * (1.0 / l)


def _rank_desc(score, axis, count):
    idx = lax.broadcasted_iota(jnp.int32, score.shape, axis)
    rank = jnp.zeros(score.shape, F32)
    for jp in range(count):
        other = score[:, jp:jp + 1] if axis == 1 else score[jp:jp + 1, :]
        beats = (other > score) | ((other == score) & (idx > jp))
        rank = rank + jnp.where(beats, 1.0, 0.0)
    return rank


MOBA_T = 512


def _moba_kernel(q_ref, k_ref, v_ref, o_ref, kmean_scr, kaug_scr, s_scr, m_scr, l_scr, acc_scr, *, nblk):
    qi = pl.program_id(2)
    t = MOBA_T
    blk_shift = MOBA_BLOCK.bit_length() - 1

    @pl.when(qi == 0)
    def _():
        kcol = lax.broadcasted_iota(jnp.int32, (t, LANES), 1)
        krow = lax.broadcasted_iota(jnp.int32, (t, LANES), 0)
        for n in range(nblk):
            kb = k_ref[n * MOBA_BLOCK:(n + 1) * MOBA_BLOCK, :].astype(F32)
            kmean_scr[n:n + 1, :] = jnp.mean(kb, axis=0, keepdims=True)
        for c in range(k_ref.shape[0] // t):
            kaug_scr[c * t:(c + 1) * t, :HEAD_DIM] = k_ref[c * t:(c + 1) * t, :]
            blk = jnp.right_shift(c * t + krow, blk_shift)
            kaug_scr[c * t:(c + 1) * t, HEAD_DIM:] = jnp.where(kcol == blk, 1.0, 0.0).astype(BF16)

    q = q_ref[...]
    gate = _dot_nt(kmean_scr[...].astype(BF16), q)
    brow = lax.broadcasted_iota(jnp.int32, gate.shape, 0)
    qblk = jnp.right_shift(qi * t + lax.broadcasted_iota(jnp.int32, gate.shape, 1), blk_shift)
    past = brow < qblk
    gate = jnp.where(past, gate, NEG)
    rank = _rank_desc(gate, 0, nblk)
    allow = (past & (rank < float(MOBA_TOPK))) | (brow == qblk)
    bias_t = jnp.where(allow, 0.0, NEG)
    bias_t = jnp.concatenate([bias_t, jnp.zeros((LANES - nblk, t), F32)], axis=0)
    q_aug = jnp.concatenate([q, bias_t.T.astype(BF16)], axis=1)

    def qk(j):
        return _dot_nt(q_aug, kaug_scr[pl.ds(pl.multiple_of(j * t, t), t), :])

    def vt(j):
        return v_ref[pl.ds(pl.multiple_of(j * t, t), t), :]

    def causal(s):
        r = lax.broadcasted_iota(jnp.int32, s.shape, 0)
        c = lax.broadcasted_iota(jnp.int32, s.shape, 1)
        return jnp.where(c <= r, s, NEG)

    o = _two_pass_attention(qk, vt, causal, qi, s_scr, m_scr, l_scr, acc_scr)
    o_ref[...] = o.astype(BF16)


def _moba(proj, bsz, seq):
    nq = seq // MOBA_T
    nblk = seq // MOBA_BLOCK
    assert nblk % 8 == 0 and nblk <= LANES
    return pl.pallas_call(
        functools.partial(_moba_kernel, nblk=nblk),
        out_shape=jax.ShapeDtypeStruct((bsz * seq, MOBA_WIDTH), BF16),
        grid=(bsz, MOBA_HEADS, nq),
        in_specs=[
            pl.BlockSpec((MOBA_T, HEAD_DIM), lambda b, h, i: (b * nq + i, CB_MQ + h)),
            pl.BlockSpec((seq, HEAD_DIM), lambda b, h, i: (b, CB_MK + h)),
            pl.BlockSpec((seq, HEAD_DIM), lambda b, h, i: (b, CB_MV + h)),
        ],
        out_specs=pl.BlockSpec((MOBA_T, HEAD_DIM), lambda b, h, i: (b * nq + i, h)),
        scratch_shapes=[
            pltpu.VMEM((nblk, HEAD_DIM), F32),
            pltpu.VMEM((seq, 2 * HEAD_DIM), BF16),
            pltpu.VMEM((nq, MOBA_T, MOBA_T), F32),
            pltpu.VMEM((MOBA_T, LANES), F32),
            pltpu.VMEM((MOBA_T, LANES), F32),
            pltpu.VMEM((MOBA_T, HEAD_DIM), F32),
        ],
        compiler_params=_cparams(("arbitrary", "arbitrary", "arbitrary")),
        name="moba",
    )(proj, proj, proj)


CMP_HALF_FEAT = CMP_STRIDE * HEAD_DIM


def _compress_kernel(x_ref, pos_ref, w1f_ref, w1_ref, w2_ref, o_ref):
    x = x_ref[0, 0]
    a = jnp.dot(x, w1_ref[:CMP_HALF_FEAT, :], preferred_element_type=F32)
    b = jnp.dot(x, w1_ref[CMP_HALF_FEAT:, :], preferred_element_type=F32)
    nch = x.shape[0]
    b_next = pltpu.roll(b, nch - 1, 0)
    pos_bias = jnp.sum(w1f_ref[...] * pos_ref[...], axis=0, keepdims=True)
    hid = jax.nn.gelu(a + b_next + pos_bias, approximate=True)
    o_ref[0, 0] = jnp.dot(hid.astype(BF16), w2_ref[...], preferred_element_type=F32).astype(BF16)


def _compress(xc, pos_col, w1, w2):
    bsz, hk, nch, feat = xc.shape
    return pl.pallas_call(
        _compress_kernel,
        out_shape=jax.ShapeDtypeStruct((bsz, hk, nch, HEAD_DIM), BF16),
        grid=(bsz, hk),
        in_specs=[
            pl.BlockSpec((1, 1, nch, feat), lambda b, k: (b, k, 0, 0)),
            pl.BlockSpec((CMP_LEN * HEAD_DIM, 1), lambda b, k: (0, 0)),
            pl.BlockSpec((CMP_LEN * HEAD_DIM, CMP_HIDDEN), lambda b, k: (0, 0)),
            pl.BlockSpec((CMP_LEN * HEAD_DIM, CMP_HIDDEN), lambda b, k: (0, 0)),
            pl.BlockSpec((CMP_HIDDEN, HEAD_DIM), lambda b, k: (0, 0)),
        ],
        out_specs=pl.BlockSpec((1, 1, nch, HEAD_DIM), lambda b, k: (b, k, 0, 0)),
        compiler_params=_cparams(("arbitrary", "arbitrary")),
        name="compress",
    )(xc, pos_col, w1, w1.astype(BF16), w2.astype(BF16))


NSA_TQ = 256
NSA_TK = 512
NSA_ROWS = NSA_GROUP * NSA_TQ


def _split3(x):
    hi = x.astype(BF16)
    r1 = x - hi.astype(F32)
    mid = r1.astype(BF16)
    lo = (r1 - mid.astype(F32)).astype(BF16)
    return hi, mid, lo


def _nsa_kernel(q_ref, kc_ref, vc_ref, ks_ref, vs_ref, kw_ref, vw_ref, gl_ref, mt_ref, o_ref,
                kaug_scr, s_scr, m_scr, l_scr, acc_scr):
    qi = pl.program_id(2)
    tq = NSA_TQ
    tk = NSA_TK
    seq = ks_ref.shape[0]
    slc_shift = SLC_BLOCK.bit_length() - 1

    @pl.when(qi == 0)
    def _():
        kcol = lax.broadcasted_iota(jnp.int32, (tk, LANES), 1)
        krow = lax.broadcasted_iota(jnp.int32, (tk, LANES), 0)
        for c in range(seq // tk):
            kaug_scr[c * tk:(c + 1) * tk, :HEAD_DIM] = ks_ref[c * tk:(c + 1) * tk, :]
            blk = jnp.right_shift(c * tk + krow, slc_shift)
            kaug_scr[c * tk:(c + 1) * tk, HEAD_DIM:] = jnp.where(kcol == blk, 1.0, 0.0).astype(BF16)

    q = q_ref[...]
    qg = jnp.concatenate([q[:, g * HEAD_DIM:(g + 1) * HEAD_DIM] for g in range(NSA_GROUP)], axis=0)
    ncmp = kc_ref.shape[2]

    def qpos_of(shape):
        return qi * tq + jnp.bitwise_and(lax.broadcasted_iota(jnp.int32, shape, 0), tq - 1)

    colc = lax.broadcasted_iota(jnp.int32, (NSA_ROWS, ncmp), 1)
    cmask = colc * CMP_STRIDE + (CMP_LEN - 1) <= qpos_of((NSA_ROWS, ncmp))
    s = jnp.where(cmask, _dot_nt(qg, kc_ref[0, 0]), NEG)
    e = jnp.exp2((s - jnp.max(s, axis=-1, keepdims=True)) * EXP2_SCALE)
    has_blk = qpos_of((NSA_ROWS, 1)) >= CMP_LEN - 1
    p = e * jnp.where(has_blk, 1.0 / jnp.sum(e, axis=-1, keepdims=True), 0.0)
    o_cmp = jnp.dot(p.astype(BF16), vc_ref[0, 0], preferred_element_type=F32)

    imp = p[0:tq]
    for g in range(1, NSA_GROUP):
        imp = imp + p[g * tq:(g + 1) * tq]
    n_slc = seq // SLC_BLOCK
    mt = mt_ref[...]
    p_slc = sum(_dot_nt(mt, part) for part in _split3(imp))
    jrow = lax.broadcasted_iota(jnp.int32, p_slc.shape, 0)
    jt = jnp.right_shift(qi * tq + lax.broadcasted_iota(jnp.int32, p_slc.shape, 1), slc_shift)
    valid = jrow <= jt
    forced = (jrow == 0) | (jrow == jt) | (jrow == jt - 1)
    score = jnp.where(valid & forced, BIG, jnp.where(valid, p_slc, NEG))
    rank = _rank_desc(score, 0, n_slc)
    sel = valid & (rank < float(min(SLC_TOPK, n_slc)))
    bias_t = jnp.where(sel, 0.0, NEG)
    bias_t = jnp.concatenate([bias_t, jnp.zeros((LANES - n_slc, tq), F32)], axis=0)
    bias = bias_t.T.astype(BF16)
    q_aug = jnp.concatenate([qg, jnp.concatenate([bias] * NSA_GROUP, axis=0)], axis=1)

    def slc_qk(j):
        return _dot_nt(q_aug, kaug_scr[pl.ds(pl.multiple_of(j * tk, tk), tk), :])

    def slc_v(j):
        return vs_ref[pl.ds(pl.multiple_of(j * tk, tk), tk), :]

    n_full = (qi * tq) // tk

    def slc_causal(s):
        kpos = n_full * tk + lax.broadcasted_iota(jnp.int32, s.shape, 1)
        return jnp.where(kpos <= qpos_of(s.shape), s, NEG)

    o_slc = _two_pass_attention(slc_qk, slc_v, slc_causal, n_full, s_scr, m_scr, l_scr, acc_scr)

    span = WINDOW + tq
    start = pl.multiple_of(jnp.maximum(qi * tq - WINDOW, 0), tq)
    s = _dot_nt(qg, kw_ref[pl.ds(start, span), :])
    kpos = start + lax.broadcasted_iota(jnp.int32, s.shape, 1)
    qpos = qpos_of(s.shape)
    s = jnp.where((kpos <= qpos) & (kpos > qpos - WINDOW), s, NEG)
    o_win = _softmax_pv(s, vw_ref[pl.ds(start, span), :])

    sig = jax.nn.sigmoid(gl_ref[...])
    for g in range(NSA_GROUP):
        rows = slice(g * tq, (g + 1) * tq)
        og = (sig[:, g:g + 1] * o_cmp[rows]
              + sig[:, NSA_GROUP + g:NSA_GROUP + g + 1] * o_slc[rows]
              + sig[:, 2 * NSA_GROUP + g:2 * NSA_GROUP + g + 1] * o_win[rows])
        o_ref[:, g * HEAD_DIM:(g + 1) * HEAD_DIM] = og.astype(BF16)


def _slc_weight_matrix(n_slc, ncmp):
    rs, rc = SLC_BLOCK // CMP_STRIDE, CMP_LEN // CMP_STRIDE
    j = jnp.arange(n_slc)[:, None]
    n = jnp.arange(ncmp)[None, :]
    w = jnp.zeros((n_slc, ncmp), F32)
    for o in range(rs + rc - 1):
        w_o = float(sum(1 for m in range(rs) for nn in range(rc) if m - nn + rc - 1 == o))
        w = w + jnp.where(n == rs * j + o - (rc - 1), w_o, 0.0)
    return w.astype(BF16)


def _nsa(proj, gates, kc, vc, bsz, seq):
    nq = seq // NSA_TQ
    ncmp = kc.shape[2]
    n_slc = seq // SLC_BLOCK
    assert n_slc % 8 == 0 and n_slc <= LANES and seq % NSA_TK == 0
    qw = NSA_GROUP * HEAD_DIM
    kv_spec = lambda cb: pl.BlockSpec((seq, HEAD_DIM), lambda b, k, i: (b, cb + k))
    cmp_spec = pl.BlockSpec((1, 1, ncmp, HEAD_DIM), lambda b, k, i: (b, k, 0, 0))
    return pl.pallas_call(
        _nsa_kernel,
        out_shape=jax.ShapeDtypeStruct((bsz * seq, NSA_WIDTH), BF16),
        grid=(bsz, NSA_KV_HEADS, nq),
        in_specs=[
            pl.BlockSpec((NSA_TQ, qw), lambda b, k, i: (b * nq + i, CB_NQ * HEAD_DIM // qw + k)),
            cmp_spec, cmp_spec,
            kv_spec(CB_KS), kv_spec(CB_VS), kv_spec(CB_KW), kv_spec(CB_VW),
            pl.BlockSpec((NSA_TQ, LANES), lambda b, k, i: (b * nq + i, k)),
            pl.BlockSpec((n_slc, ncmp), lambda b, k, i: (0, 0)),
        ],
        out_specs=pl.BlockSpec((NSA_TQ, qw), lambda b, k, i: (b * nq + i, k)),
        scratch_shapes=[
            pltpu.VMEM((seq, 2 * HEAD_DIM), BF16),
            pltpu.VMEM((seq // NSA_TK, NSA_ROWS, NSA_TK), F32),
            pltpu.VMEM((NSA_ROWS, LANES), F32),
            pltpu.VMEM((NSA_ROWS, LANES), F32),
            pltpu.VMEM((NSA_ROWS, HEAD_DIM), F32),
        ],
        compiler_params=_cparams(("arbitrary", "arbitrary", "arbitrary")),
        name="nsa",
    )(proj, kc, vc, proj, proj, proj, proj, gates, _slc_weight_matrix(n_slc, ncmp))


OUT_TM = 512


def _outproj_kernel(om_ref, on_ref, wm_ref, wn_ref, x_ref, ga_ref, npost_ref, npre_ref, sh_ref, sc_ref,
                    x1_ref, h2_ref):
    y = (jnp.dot(om_ref[...], wm_ref[...], preferred_element_type=F32)
         + jnp.dot(on_ref[...], wn_ref[...], preferred_element_type=F32))
    x1 = x_ref[...] + ga_ref[0] * _rms(y, npost_ref[...])
    x1_ref[...] = x1
    h2_ref[...] = (_rms(x1, npre_ref[...]) * (1.0 + sc_ref[0]) + sh_ref[0]).astype(BF16)


def _outproj(o_moba, o_nsa, w_out, x2d, mod3, npost, npre, seq):
    rows = x2d.shape[0]
    tpb = seq // OUT_TM
    mod_spec = lambda idx: pl.BlockSpec((1, 1, D_MODEL), lambda i: ((i // tpb) * 6 + idx, 0, 0))
    vec_spec = pl.BlockSpec((1, D_MODEL), lambda i: (0, 0))
    w_bf = w_out.astype(BF16)
    return pl.pallas_call(
        _outproj_kernel,
        out_shape=(jax.ShapeDtypeStruct((rows, D_MODEL), F32),
                   jax.ShapeDtypeStruct((rows, D_MODEL), BF16)),
        grid=(rows // OUT_TM,),
        in_specs=[
            pl.BlockSpec((OUT_TM, MOBA_WIDTH), lambda i: (i, 0)),
            pl.BlockSpec((OUT_TM, NSA_WIDTH), lambda i: (i, 0)),
            pl.BlockSpec((MOBA_WIDTH, D_MODEL), lambda i: (0, 0)),
            pl.BlockSpec((NSA_WIDTH, D_MODEL), lambda i: (0, 0)),
            pl.BlockSpec((OUT_TM, D_MODEL), lambda i: (i, 0)),
            mod_spec(2), vec_spec, vec_spec, mod_spec(3), mod_spec(4),
        ],
        out_specs=(pl.BlockSpec((OUT_TM, D_MODEL), lambda i: (i, 0)),
                   pl.BlockSpec((OUT_TM, D_MODEL), lambda i: (i, 0))),
        compiler_params=_cparams(("arbitrary",)),
        name="outproj",
    )(o_moba, o_nsa, w_bf[:MOBA_WIDTH], w_bf[MOBA_WIDTH:], x2d, mod3, npost, npre, mod3, mod3)


FFN_TM = 512
FFN_TF = 512
FFN_HALO = 16
CONV_WIDTH = 3


def _ffn_kernel(h_ref, halo_ref, wg_ref, wv_ref, cwg_ref, cwv_ref, cbg_ref, cbv_ref, wd_ref,
                x1_ref, gf_ref, npost_ref, o_ref, acc_scr, u_scr, *, tiles_per_seq):
    i = pl.program_id(0)
    j = pl.program_id(1)
    halo = halo_ref[...]
    halo = jnp.where(i % tiles_per_seq == 0, jnp.zeros_like(halo), halo)
    h_aug = jnp.concatenate([halo, h_ref[...]], axis=0)

    def conv_branch(w_ref, cw_ref, cb_ref):
        u_scr[...] = jnp.dot(h_aug, w_ref[...], preferred_element_type=F32)
        cw = cw_ref[...]
        out = cb_ref[...]
        for tap in range(CONV_WIDTH):
            ofs = FFN_HALO - (CONV_WIDTH - 1) + tap
            out = out + cw[tap:tap + 1, :] * u_scr[pl.ds(ofs, FFN_TM), :]
        return out

    gate = conv_branch(wg_ref, cwg_ref, cbg_ref)
    val = conv_branch(wv_ref, cwv_ref, cbv_ref)
    act = (jax.nn.gelu(gate, approximate=True) * val).astype(BF16)
    contrib = jnp.dot(act, wd_ref[...], preferred_element_type=F32)

    @pl.when(j == 0)
    def _():
        acc_scr[...] = contrib

    @pl.when(j > 0)
    def _():
        acc_scr[...] += contrib

    @pl.when(j == pl.num_programs(1) - 1)
    def _():
        o_ref[...] = x1_ref[...] + gf_ref[0] * _rms(acc_scr[...], npost_ref[...])


def _ffn(h2, x1, w_up, conv_w, conv_b, w_down, mod3, npost, seq):
    rows = h2.shape[0]
    tps = seq // FFN_TM
    nj = D_FF // FFN_TF
    w_up_bf = w_up.astype(BF16)
    cb = conv_b.reshape(1, 2 * D_FF)
    return pl.pallas_call(
        functools.partial(_ffn_kernel, tiles_per_seq=tps),
        out_shape=jax.ShapeDtypeStruct((rows, D_MODEL), F32),
        grid=(rows // FFN_TM, nj),
        in_specs=[
            pl.BlockSpec((FFN_TM, D_MODEL), lambda i, j: (i, 0)),
            pl.BlockSpec((FFN_HALO, D_MODEL),
                         lambda i, j: (jnp.maximum(i * (FFN_TM // FFN_HALO) - 1, 0), 0)),
            pl.BlockSpec((D_MODEL, FFN_TF), lambda i, j: (0, j)),
            pl.BlockSpec((D_MODEL, FFN_TF), lambda i, j: (0, nj + j)),
            pl.BlockSpec((CONV_WIDTH, FFN_TF), lambda i, j: (0, j)),
            pl.BlockSpec((CONV_WIDTH, FFN_TF), lambda i, j: (0, nj + j)),
            pl.BlockSpec((1, FFN_TF), lambda i, j: (0, j)),
            pl.BlockSpec((1, FFN_TF), lambda i, j: (0, nj + j)),
            pl.BlockSpec((FFN_TF, D_MODEL), lambda i, j: (j, 0)),
            pl.BlockSpec((FFN_TM, D_MODEL), lambda i, j: (i, 0)),
            pl.BlockSpec((1, 1, D_MODEL), lambda i, j: ((i // tps) * 6 + 5, 0, 0)),
            pl.BlockSpec((1, D_MODEL), lambda i, j: (0, 0)),
        ],
        out_specs=pl.BlockSpec((FFN_TM, D_MODEL), lambda i, j: (i, 0)),
        scratch_shapes=[
            pltpu.VMEM((FFN_TM, D_MODEL), F32),
            pltpu.VMEM((FFN_HALO + FFN_TM, FFN_TF), F32),
        ],
        compiler_params=_cparams(("arbitrary", "arbitrary")),
        name="ffn",
    )(h2, h2, w_up_bf, w_up_bf, conv_w, conv_w, cb, cb, w_down.astype(BF16), x1, mod3, npost)


def _reorder_w_in(w_in):
    c = 0
    parts = {}
    for name, width in (("mq", MOBA_WIDTH), ("mk", MOBA_WIDTH), ("mv", MOBA_WIDTH), ("nq", NSA_WIDTH),
                        ("kc", KV_WIDTH), ("vc", KV_WIDTH), ("ks", KV_WIDTH), ("vs", KV_WIDTH),
                        ("kw", KV_WIDTH), ("vw", KV_WIDTH), ("ng", 3 * NSA_HEADS)):
        parts[name] = w_in[:, c:c + width]
        c += width
    ng = parts["ng"].reshape(D_MODEL, 3, NSA_KV_HEADS, NSA_GROUP).transpose(0, 2, 1, 3)
    ng = ng.reshape(D_MODEL, NSA_KV_HEADS, 3 * NSA_GROUP)
    ng = jnp.pad(ng, ((0, 0), (0, 0), (0, LANES - 3 * NSA_GROUP))).reshape(D_MODEL, GATE_COLS)
    rope = [parts[n] for n in ("mq", "mk", "nq", "kc", "ks", "kw")]
    rope_w = sum(p.shape[1] for p in rope)
    rope.append(jnp.zeros((D_MODEL, ROPE_COLS - rope_w), w_in.dtype))
    plain = [parts[n] for n in ("mv", "vc", "vs", "vw")] + [ng]
    return jnp.concatenate(rope + plain, axis=1).astype(BF16)


def _chunks(proj, cb, bsz, seq):
    a = proj[:, cb * HEAD_DIM:(cb + NSA_KV_HEADS) * HEAD_DIM]
    a = a.reshape(bsz, seq // CMP_STRIDE, CMP_STRIDE, NSA_KV_HEADS, HEAD_DIM).transpose(0, 3, 1, 2, 4)
    return a.reshape(bsz, NSA_KV_HEADS, seq // CMP_STRIDE, CMP_HALF_FEAT)


def kernel(x, c, positions, w_ada, b_ada, norm_pre_mix, norm_post_mix, norm_pre_ffn, norm_post_ffn,
           w_in, w_out, cmp_pos_k, cmp_w1_k, cmp_w2_k, cmp_pos_v, cmp_w1_v, cmp_w2_v,
           w_up, conv_w, conv_b, w_down):
    bsz, seq, d = x.shape
    depth = w_in.shape[0]
    assert d == D_MODEL and seq % WINDOW == 0 and seq % PROJ_TM == 0
    cos, sin = _rope_tables(positions)
    x2d = x.reshape(bsz * seq, d)
    for l in range(depth):
        mod3 = _ada(c, w_ada[l], b_ada[l]).reshape(bsz * 6, 1, d)
        proj, gates = _inproj(x2d, norm_pre_mix[l].reshape(1, d), mod3, cos, sin,
                              _reorder_w_in(w_in[l]), seq)
        o_moba = _moba(proj, bsz, seq)
        kc = _compress(_chunks(proj, CB_KC, bsz, seq), cmp_pos_k[l].reshape(-1, 1), cmp_w1_k[l], cmp_w2_k[l])
        vc = _compress(_chunks(proj, CB_VC, bsz, seq), cmp_pos_v[l].reshape(-1, 1), cmp_w1_v[l], cmp_w2_v[l])
        o_nsa = _nsa(proj, gates, kc, vc, bsz, seq)
        x1, h2 = _outproj(o_moba, o_nsa, w_out[l], x2d, mod3, norm_post_mix[l].reshape(1, d),
                          norm_pre_ffn[l].reshape(1, d), seq)
        x2d = _ffn(h2, x1, w_up[l], conv_w[l], conv_b[l], w_down[l], mod3,
                   norm_post_ffn[l].reshape(1, d), seq)
    return x2d.reshape(bsz, seq, d)
```

```python
import functools

import jax
import jax.numpy as jnp
from jax import lax
from jax.experimental import pallas as pl
from jax.experimental.pallas import tpu as pltpu

F32 = jnp.float32
BF16 = jnp.bfloat16

D_MODEL = 2048
HEAD_DIM = 128
HALF = HEAD_DIM // 2
MOBA_HEADS = 8
NSA_HEADS = 8
NSA_KV_HEADS = 2
NSA_GROUP = NSA_HEADS // NSA_KV_HEADS
MOBA_BLOCK = 256
MOBA_TOPK = 3
CMP_LEN = 32
CMP_STRIDE = 16
CMP_HIDDEN = 2 * HEAD_DIM
SLC_BLOCK = 64
SLC_TOPK = 16
WINDOW = 512
D_FF = 4 * D_MODEL
ROPE_THETA = 10000.0
EPS = 1e-6
NEG = -1e30
BIG = 1e9
SCALE = HEAD_DIM ** -0.5

MOBA_WIDTH = MOBA_HEADS * HEAD_DIM
NSA_WIDTH = NSA_HEADS * HEAD_DIM
KV_WIDTH = NSA_KV_HEADS * HEAD_DIM

LANES = 128
MXU_DIM = 256
VMEM_LIMIT = 56 * 1024 * 1024

CB_MQ, CB_MK, CB_MV, CB_NQ = 0, 8, 16, 24
CB_KC, CB_VC, CB_KS, CB_VS, CB_KW, CB_VW = 32, 34, 36, 38, 40, 42
PROJ_GROUPS = 44
PROJ_COLS = PROJ_GROUPS * HEAD_DIM
PROJ_COLS_PADDED = PROJ_COLS + LANES
ROPE_GROUPS = frozenset(list(range(CB_MQ, CB_MV)) + list(range(CB_NQ, CB_VC))
                        + [CB_KS, CB_KS + 1, CB_KW, CB_KW + 1])
assert PROJ_COLS % MXU_DIM == 0


def _cparams(sem):
    return pltpu.CompilerParams(dimension_semantics=sem, vmem_limit_bytes=VMEM_LIMIT)


def _dot_nt(a, b):
    return lax.dot_general(a, b, (((1,), (1,)), ((), ())), preferred_element_type=F32)


def _rms(x, w):
    ms = jnp.mean(x * x, axis=-1, keepdims=True)
    return (x * lax.rsqrt(ms + EPS)) * w


ADA_TN = 2048
ADA_TK = 512


def _ada_kernel(ct_ref, w_ref, b_ref, o_ref):
    k = pl.program_id(1)

    @pl.when(k == 0)
    def _():
        o_ref[...] = jnp.broadcast_to(b_ref[...], o_ref.shape)

    ct = ct_ref[...]
    s = ct * jax.nn.sigmoid(ct)
    w = w_ref[...]
    rows = [jnp.sum(w * s[:, b:b + 1], axis=0, keepdims=True) for b in range(ct.shape[1])]
    o_ref[...] += jnp.concatenate(rows, axis=0)


def _ada(c, w_ada, b_ada):
    bsz = c.shape[0]
    n = w_ada.shape[1]
    return pl.pallas_call(
        _ada_kernel,
        out_shape=jax.ShapeDtypeStruct((bsz, n), F32),
        grid=(n // ADA_TN, D_MODEL // ADA_TK),
        in_specs=[
            pl.BlockSpec((ADA_TK, bsz), lambda j, k: (k, 0)),
            pl.BlockSpec((ADA_TK, ADA_TN), lambda j, k: (k, j)),
            pl.BlockSpec((1, ADA_TN), lambda j, k: (0, j)),
        ],
        out_specs=pl.BlockSpec((bsz, ADA_TN), lambda j, k: (0, j)),
        compiler_params=_cparams(("arbitrary", "arbitrary")),
        name="ada",
    )(c.T, w_ada, b_ada.reshape(1, n))


ROPE_TS = 1024


def _rope_kernel(pos_ref, inv_ref, cos_ref, sin_ref):
    ang = pos_ref[...].astype(F32) * inv_ref[...]
    lane = lax.broadcasted_iota(jnp.int32, ang.shape, 1)
    cos_ref[...] = jnp.cos(ang)
    sn = jnp.sin(ang)
    sin_ref[...] = jnp.where(lane < HALF, -sn, sn)


def _rope_tables(positions):
    rows = positions.size
    inv = ROPE_THETA ** (-jnp.arange(HALF, dtype=F32) / HALF)
    inv = jnp.concatenate([inv, inv]).reshape(1, HEAD_DIM)
    return pl.pallas_call(
        _rope_kernel,
        out_shape=(jax.ShapeDtypeStruct((rows, HEAD_DIM), F32),) * 2,
        grid=(rows // ROPE_TS,),
        in_specs=[
            pl.BlockSpec((ROPE_TS, 1), lambda i: (i, 0)),
            pl.BlockSpec((1, HEAD_DIM), lambda i: (0, 0)),
        ],
        out_specs=(pl.BlockSpec((ROPE_TS, HEAD_DIM), lambda i: (i, 0)),) * 2,
        compiler_params=_cparams(("arbitrary",)),
        name="rope_tables",
    )(positions.reshape(rows, 1), inv)


PROJ_TM = 512


def _inproj_kernel(x_ref, nw_ref, sh_ref, sc_ref, cos_ref, sin_ref, w_ref, o_ref, g_ref):
    h = (_rms(x_ref[...], nw_ref[...]) * (1.0 + sc_ref[0]) + sh_ref[0]).astype(BF16)
    cos = cos_ref[...]
    sin = sin_ref[...]
    for c in range(PROJ_COLS // MXU_DIM):
        acc = jnp.dot(h, w_ref[:, c * MXU_DIM:(c + 1) * MXU_DIM], preferred_element_type=F32)
        for g in range(MXU_DIM // HEAD_DIM):
            grp = c * (MXU_DIM // HEAD_DIM) + g
            y = acc[:, g * HEAD_DIM:(g + 1) * HEAD_DIM]
            if grp in ROPE_GROUPS:
                y = y * cos + pltpu.roll(y, HALF, 1) * sin
            o_ref[:, grp * HEAD_DIM:(grp + 1) * HEAD_DIM] = y.astype(BF16)
    g_ref[...] = jnp.dot(h, w_ref[:, PROJ_COLS:], preferred_element_type=F32)


def _inproj(x2d, nw, mod3, cos, sin, w_in, seq):
    rows = x2d.shape[0]
    tpb = seq // PROJ_TM
    w_pad = jnp.pad(w_in.astype(BF16), ((0, 0), (0, PROJ_COLS_PADDED - w_in.shape[1])))
    return pl.pallas_call(
        _inproj_kernel,
        out_shape=(jax.ShapeDtypeStruct((rows, PROJ_COLS), BF16),
                   jax.ShapeDtypeStruct((rows, LANES), F32)),
        grid=(rows // PROJ_TM,),
        in_specs=[
            pl.BlockSpec((PROJ_TM, D_MODEL), lambda i: (i, 0)),
            pl.BlockSpec((1, D_MODEL), lambda i: (0, 0)),
            pl.BlockSpec((1, 1, D_MODEL), lambda i: ((i // tpb) * 6 + 0, 0, 0)),
            pl.BlockSpec((1, 1, D_MODEL), lambda i: ((i // tpb) * 6 + 1, 0, 0)),
            pl.BlockSpec((PROJ_TM, HEAD_DIM), lambda i: (i, 0)),
            pl.BlockSpec((PROJ_TM, HEAD_DIM), lambda i: (i, 0)),
            pl.BlockSpec((D_MODEL, PROJ_COLS_PADDED), lambda i: (0, 0), pipeline_mode=pl.Buffered(1)),
        ],
        out_specs=(pl.BlockSpec((PROJ_TM, PROJ_COLS), lambda i: (i, 0)),
                   pl.BlockSpec((PROJ_TM, LANES), lambda i: (i, 0))),
        compiler_params=_cparams(("arbitrary",)),
        name="inproj",
    )(x2d, nw, mod3, mod3, cos, sin, w_pad)


EXP2_SCALE = SCALE * 1.4426950408889634


def _lane_tiles(x):
    return [x[:, t * LANES:(t + 1) * LANES] for t in range(x.shape[1] // LANES)]


def _lane_fold(x, op):
    return functools.reduce(op, _lane_tiles(x))


def _softmax_pv(s, v):
    m = jnp.max(s, axis=-1, keepdims=True)
    p = jnp.exp2((s - m) * EXP2_SCALE)
    l = jnp.sum(p, axis=-1, keepdims=True)
    return jnp.dot(p.astype(BF16), v, preferred_element_type=F32) * (1.0 / l)


def _two_pass_attention(qk_fn, v_fn, last_mask_fn, n_full, s_scr, m_scr, l_scr, acc_scr):
    m_scr[...] = jnp.full(m_scr.shape, NEG, F32)

    def pass1_tile(j):
        s = qk_fn(j)
        s_scr[j] = s
        return _lane_fold(s, jnp.maximum)

    def pass1(pair, carry):
        both = jnp.maximum(pass1_tile(2 * pair), pass1_tile(2 * pair + 1))
        m_scr[...] = jnp.maximum(m_scr[...], both)
        return carry

    lax.fori_loop(0, n_full // 2, pass1, 0)

    @pl.when(n_full % 2 == 1)
    def _():
        m_scr[...] = jnp.maximum(m_scr[...], pass1_tile(n_full - 1))

    s = last_mask_fn(qk_fn(n_full))
    s_scr[n_full] = s
    m = jnp.max(jnp.maximum(m_scr[...], _lane_fold(s, jnp.maximum)), axis=-1, keepdims=True)
    m_scr[...] = jnp.broadcast_to(m, m_scr.shape)
    l_scr[...] = jnp.zeros(l_scr.shape, F32)
    acc_scr[...] = jnp.zeros(acc_scr.shape, F32)

    def pass2_tile(j):
        mb = m_scr[...]
        ps = [jnp.exp2((st - mb) * EXP2_SCALE) for st in _lane_tiles(s_scr[j])]
        p = jnp.concatenate(ps, axis=1).astype(BF16)
        return functools.reduce(jnp.add, ps), jnp.dot(p, v_fn(j), preferred_element_type=F32)

    def pass2(pair, carry):
        l0, a0 = pass2_tile(2 * pair)
        l1, a1 = pass2_tile(2 * pair + 1)
        l_scr[...] += l0 + l1
        acc_scr[...] += a0 + a1
        return carry

    n_tiles = n_full + 1
    lax.fori_loop(0, n_tiles // 2, pass2, 0)

    @pl.when(n_tiles % 2 == 1)
    def _():
        l0, a0 = pass2_tile(n_tiles - 1)
        l_scr[...] += l0
        acc_scr[...] += a0

    l = jnp.sum(l_scr[...], axis=-1, keepdims=True)
    return acc_scr[...] * (1.0 / l)


def _rank_desc(score, axis, count):
    idx = lax.broadcasted_iota(jnp.int32, score.shape, axis)
    rank = jnp.zeros(score.shape, F32)
    for jp in range(count):
        other = score[:, jp:jp + 1] if axis == 1 else score[jp:jp + 1, :]
        beats = (other > score) | ((other == score) & (idx > jp))
        rank = rank + jnp.where(beats, 1.0, 0.0)
    return rank


MOBA_T = 1024
MOBA_SETUP_CH = 512


def _moba_kernel(q_ref, k_ref, v_ref, o_ref, kmean_scr, qaug_scr, kaug_scr, s_scr, m_scr, l_scr, acc_scr,
                 *, nblk):
    qi = pl.program_id(2)
    t = MOBA_T
    blk_shift = MOBA_BLOCK.bit_length() - 1

    @pl.when(qi == 0)
    def _():
        ch = MOBA_SETUP_CH
        kcol = lax.broadcasted_iota(jnp.int32, (ch, LANES), 1)
        krow = lax.broadcasted_iota(jnp.int32, (ch, LANES), 0)
        for n in range(nblk):
            kb = k_ref[n * MOBA_BLOCK:(n + 1) * MOBA_BLOCK, :].astype(F32)
            kmean_scr[n:n + 1, :] = jnp.mean(kb, axis=0, keepdims=True)
        kmean = kmean_scr[...].astype(BF16)
        brow = lax.broadcasted_iota(jnp.int32, (nblk, ch), 0)
        qlane = lax.broadcasted_iota(jnp.int32, (nblk, ch), 1)
        for c in range(k_ref.shape[0] // ch):
            rows = slice(c * ch, (c + 1) * ch)
            kaug_scr[rows, :HEAD_DIM] = k_ref[rows, :]
            blk = jnp.right_shift(c * ch + krow, blk_shift)
            kaug_scr[rows, HEAD_DIM:] = jnp.where(kcol == blk, 1.0, 0.0).astype(BF16)
            q = q_ref[rows, :]
            qblk = jnp.right_shift(c * ch + qlane, blk_shift)
            past = brow < qblk
            gate = jnp.where(past, _dot_nt(kmean, q), NEG)
            rank = _rank_desc(gate, 0, nblk)
            allow = (past & (rank < float(MOBA_TOPK))) | (brow == qblk)
            bias_t = jnp.where(allow, 0.0, NEG)
            bias_t = jnp.concatenate([bias_t, jnp.zeros((LANES - nblk, ch), F32)], axis=0)
            qaug_scr[rows, :HEAD_DIM] = q
            qaug_scr[rows, HEAD_DIM:] = bias_t.T.astype(BF16)

    q_aug = qaug_scr[pl.ds(pl.multiple_of(qi * t, t), t), :]

    def qk(j):
        return _dot_nt(q_aug, kaug_scr[pl.ds(pl.multiple_of(j * t, t), t), :])

    def vt(j):
        return v_ref[pl.ds(pl.multiple_of(j * t, t), t), :]

    def causal(s):
        r = lax.broadcasted_iota(jnp.int32, s.shape, 0)
        c = lax.broadcasted_iota(jnp.int32, s.shape, 1)
        return jnp.where(c <= r, s, NEG)

    o = _two_pass_attention(qk, vt, causal, qi, s_scr, m_scr, l_scr, acc_scr)
    o_ref[...] = o.astype(BF16)


def _moba(proj, bsz, seq):
    nq = seq // MOBA_T
    nblk = seq // MOBA_BLOCK
    assert nblk % 8 == 0 and nblk <= LANES
    return pl.pallas_call(
        functools.partial(_moba_kernel, nblk=nblk),
        out_shape=jax.ShapeDtypeStruct((bsz * seq, MOBA_WIDTH), BF16),
        grid=(bsz, MOBA_HEADS, nq),
        in_specs=[
            pl.BlockSpec((seq, HEAD_DIM), lambda b, h, i: (b, CB_MQ + h)),
            pl.BlockSpec((seq, HEAD_DIM), lambda b, h, i: (b, CB_MK + h)),
            pl.BlockSpec((seq, HEAD_DIM), lambda b, h, i: (b, CB_MV + h)),
        ],
        out_specs=pl.BlockSpec((MOBA_T, HEAD_DIM), lambda b, h, i: (b * nq + i, h)),
        scratch_shapes=[
            pltpu.VMEM((nblk, HEAD_DIM), F32),
            pltpu.VMEM((seq, 2 * HEAD_DIM), BF16),
            pltpu.VMEM((seq, 2 * HEAD_DIM), BF16),
            pltpu.VMEM((nq, MOBA_T, MOBA_T), F32),
            pltpu.VMEM((MOBA_T, LANES), F32),
            pltpu.VMEM((MOBA_T, LANES), F32),
            pltpu.VMEM((MOBA_T, HEAD_DIM), F32),
        ],
        compiler_params=_cparams(("arbitrary", "arbitrary", "arbitrary")),
        name="moba",
    )(proj, proj, proj)


CMP_HALF_FEAT = CMP_STRIDE * HEAD_DIM


def _compress_kernel(x_ref, pos_ref, w1f_ref, w1_ref, w2_ref, o_ref):
    x = x_ref[0, 0]
    a = jnp.dot(x, w1_ref[:CMP_HALF_FEAT, :], preferred_element_type=F32)
    b = jnp.dot(x, w1_ref[CMP_HALF_FEAT:, :], preferred_element_type=F32)
    nch = x.shape[0]
    b_next = pltpu.roll(b, nch - 1, 0)
    pos_bias = jnp.sum(w1f_ref[...] * pos_ref[...], axis=0, keepdims=True)
    hid = jax.nn.gelu(a + b_next + pos_bias, approximate=True)
    o_ref[0, 0] = jnp.dot(hid.astype(BF16), w2_ref[...], preferred_element_type=F32).astype(BF16)


def _compress(xc, pos_col, w1, w2):
    bsz, hk, nch, feat = xc.shape
    return pl.pallas_call(
        _compress_kernel,
        out_shape=jax.ShapeDtypeStruct((bsz, hk, nch, HEAD_DIM), BF16),
        grid=(bsz, hk),
        in_specs=[
            pl.BlockSpec((1, 1, nch, feat), lambda b, k: (b, k, 0, 0)),
            pl.BlockSpec((CMP_LEN * HEAD_DIM, 1), lambda b, k: (0, 0)),
            pl.BlockSpec((CMP_LEN * HEAD_DIM, CMP_HIDDEN), lambda b, k: (0, 0)),
            pl.BlockSpec((CMP_LEN * HEAD_DIM, CMP_HIDDEN), lambda b, k: (0, 0)),
            pl.BlockSpec((CMP_HIDDEN, HEAD_DIM), lambda b, k: (0, 0)),
        ],
        out_specs=pl.BlockSpec((1, 1, nch, HEAD_DIM), lambda b, k: (b, k, 0, 0)),
        compiler_params=_cparams(("arbitrary", "arbitrary")),
        name="compress",
    )(xc, pos_col, w1, w1.astype(BF16), w2.astype(BF16))


NSA_TQ = 256
NSA_TK = 512
NSA_ROWS = NSA_GROUP * NSA_TQ


def _split3(x):
    hi = x.astype(BF16)
    r1 = x - hi.astype(F32)
    mid = r1.astype(BF16)
    lo = (r1 - mid.astype(F32)).astype(BF16)
    return hi, mid, lo


def _nsa_kernel(q_ref, kc_ref, vc_ref, ks_ref, vs_ref, kw_ref, vw_ref, gl_ref, mt_ref, o_ref,
                kaug_scr, s_scr, m_scr, l_scr, acc_scr):
    qi = pl.program_id(2)
    tq = NSA_TQ
    tk = NSA_TK
    seq = ks_ref.shape[0]
    slc_shift = SLC_BLOCK.bit_length() - 1

    @pl.when(qi == 0)
    def _():
        kcol = lax.broadcasted_iota(jnp.int32, (tk, LANES), 1)
        krow = lax.broadcasted_iota(jnp.int32, (tk, LANES), 0)
        for c in range(seq // tk):
            kaug_scr[c * tk:(c + 1) * tk, :HEAD_DIM] = ks_ref[c * tk:(c + 1) * tk, :]
            blk = jnp.right_shift(c * tk + krow, slc_shift)
            kaug_scr[c * tk:(c + 1) * tk, HEAD_DIM:] = jnp.where(kcol == blk, 1.0, 0.0).astype(BF16)

    q = q_ref[...]
    qg = jnp.concatenate([q[:, g * HEAD_DIM:(g + 1) * HEAD_DIM] for g in range(NSA_GROUP)], axis=0)
    ncmp = kc_ref.shape[2]

    def qpos_of(shape):
        return qi * tq + jnp.bitwise_and(lax.broadcasted_iota(jnp.int32, shape, 0), tq - 1)

    colc = lax.broadcasted_iota(jnp.int32, (NSA_ROWS, ncmp), 1)
    cmask = colc * CMP_STRIDE + (CMP_LEN - 1) <= qpos_of((NSA_ROWS, ncmp))
    s = jnp.where(cmask, _dot_nt(qg, kc_ref[0, 0]), NEG)
    e = jnp.exp2((s - jnp.max(s, axis=-1, keepdims=True)) * EXP2_SCALE)
    has_blk = qpos_of((NSA_ROWS, 1)) >= CMP_LEN - 1
    p = e * jnp.where(has_blk, 1.0 / jnp.sum(e, axis=-1, keepdims=True), 0.0)
    o_cmp = jnp.dot(p.astype(BF16), vc_ref[0, 0], preferred_element_type=F32)

    imp = p[0:tq]
    for g in range(1, NSA_GROUP):
        imp = imp + p[g * tq:(g + 1) * tq]
    n_slc = seq // SLC_BLOCK
    mt = mt_ref[...]
    p_slc = sum(_dot_nt(mt, part) for part in _split3(imp))
    jrow = lax.broadcasted_iota(jnp.int32, p_slc.shape, 0)
    jt = jnp.right_shift(qi * tq + lax.broadcasted_iota(jnp.int32, p_slc.shape, 1), slc_shift)
    valid = jrow <= jt
    forced = (jrow == 0) | (jrow == jt) | (jrow == jt - 1)
    score = jnp.where(valid & forced, BIG, jnp.where(valid, p_slc, NEG))
    rank = _rank_desc(score, 0, n_slc)
    sel = valid & (rank < float(min(SLC_TOPK, n_slc)))
    bias_t = jnp.where(sel, 0.0, NEG)
    bias_t = jnp.concatenate([bias_t, jnp.zeros((LANES - n_slc, tq), F32)], axis=0)
    bias = bias_t.T.astype(BF16)
    q_aug = jnp.concatenate([qg, jnp.concatenate([bias] * NSA_GROUP, axis=0)], axis=1)

    def slc_qk(j):
        return _dot_nt(q_aug, kaug_scr[pl.ds(pl.multiple_of(j * tk, tk), tk), :])

    def slc_v(j):
        return vs_ref[pl.ds(pl.multiple_of(j * tk, tk), tk), :]

    n_full = (qi * tq) // tk

    def slc_causal(s):
        kpos = n_full * tk + lax.broadcasted_iota(jnp.int32, s.shape, 1)
        return jnp.where(kpos <= qpos_of(s.shape), s, NEG)

    o_slc = _two_pass_attention(slc_qk, slc_v, slc_causal, n_full, s_scr, m_scr, l_scr, acc_scr)

    span = WINDOW + tq
    start = pl.multiple_of(jnp.maximum(qi * tq - WINDOW, 0), tq)
    s = _dot_nt(qg, kw_ref[pl.ds(start, span), :])
    kpos = start + lax.broadcasted_iota(jnp.int32, s.shape, 1)
    qpos = qpos_of(s.shape)
    s = jnp.where((kpos <= qpos) & (kpos > qpos - WINDOW), s, NEG)
    o_win = _softmax_pv(s, vw_ref[pl.ds(start, span), :])

    sig = jax.nn.sigmoid(gl_ref[...])
    kv_head = pl.program_id(1)

    def gate_col(branch, g):
        cols = [sig[:, branch * NSA_HEADS + k * NSA_GROUP + g:branch * NSA_HEADS + k * NSA_GROUP + g + 1]
                for k in range(NSA_KV_HEADS)]
        col = cols[0]
        for k in range(1, NSA_KV_HEADS):
            col = jnp.where(kv_head == k, cols[k], col)
        return col

    for g in range(NSA_GROUP):
        rows = slice(g * tq, (g + 1) * tq)
        og = (gate_col(0, g) * o_cmp[rows] + gate_col(1, g) * o_slc[rows] + gate_col(2, g) * o_win[rows])
        o_ref[:, g * HEAD_DIM:(g + 1) * HEAD_DIM] = og.astype(BF16)


def _slc_weight_matrix(n_slc, ncmp):
    rs, rc = SLC_BLOCK // CMP_STRIDE, CMP_LEN // CMP_STRIDE
    j = jnp.arange(n_slc)[:, None]
    n = jnp.arange(ncmp)[None, :]
    w = jnp.zeros((n_slc, ncmp), F32)
    for o in range(rs + rc - 1):
        w_o = float(sum(1 for m in range(rs) for nn in range(rc) if m - nn + rc - 1 == o))
        w = w + jnp.where(n == rs * j + o - (rc - 1), w_o, 0.0)
    return w.astype(BF16)


def _nsa(proj, gates, kc, vc, bsz, seq):
    nq = seq // NSA_TQ
    ncmp = kc.shape[2]
    n_slc = seq // SLC_BLOCK
    assert n_slc % 8 == 0 and n_slc <= LANES and seq % NSA_TK == 0
    qw = NSA_GROUP * HEAD_DIM
    kv_spec = lambda cb: pl.BlockSpec((seq, HEAD_DIM), lambda b, k, i: (b, cb + k))
    cmp_spec = pl.BlockSpec((1, 1, ncmp, HEAD_DIM), lambda b, k, i: (b, k, 0, 0))
    return pl.pallas_call(
        _nsa_kernel,
        out_shape=jax.ShapeDtypeStruct((bsz * seq, NSA_WIDTH), BF16),
        grid=(bsz, NSA_KV_HEADS, nq),
        in_specs=[
            pl.BlockSpec((NSA_TQ, qw), lambda b, k, i: (b * nq + i, CB_NQ * HEAD_DIM // qw + k)),
            cmp_spec, cmp_spec,
            kv_spec(CB_KS), kv_spec(CB_VS), kv_spec(CB_KW), kv_spec(CB_VW),
            pl.BlockSpec((NSA_TQ, LANES), lambda b, k, i: (b * nq + i, 0)),
            pl.BlockSpec((n_slc, ncmp), lambda b, k, i: (0, 0)),
        ],
        out_specs=pl.BlockSpec((NSA_TQ, qw), lambda b, k, i: (b * nq + i, k)),
        scratch_shapes=[
            pltpu.VMEM((seq, 2 * HEAD_DIM), BF16),
            pltpu.VMEM((seq // NSA_TK, NSA_ROWS, NSA_TK), F32),
            pltpu.VMEM((NSA_ROWS, LANES), F32),
            pltpu.VMEM((NSA_ROWS, LANES), F32),
            pltpu.VMEM((NSA_ROWS, HEAD_DIM), F32),
        ],
        compiler_params=_cparams(("arbitrary", "arbitrary", "arbitrary")),
        name="nsa",
    )(proj, kc, vc, proj, proj, proj, proj, gates, _slc_weight_matrix(n_slc, ncmp))


OUT_TM = 512


def _outproj_kernel(om_ref, on_ref, wm_ref, wn_ref, x_ref, ga_ref, npost_ref, npre_ref, sh_ref, sc_ref,
                    x1_ref, h2_ref):
    y = (jnp.dot(om_ref[...], wm_ref[...], preferred_element_type=F32)
         + jnp.dot(on_ref[...], wn_ref[...], preferred_element_type=F32))
    x1 = x_ref[...] + ga_ref[0] * _rms(y, npost_ref[...])
    x1_ref[...] = x1
    h2_ref[...] = (_rms(x1, npre_ref[...]) * (1.0 + sc_ref[0]) + sh_ref[0]).astype(BF16)


def _outproj(o_moba, o_nsa, w_out, x2d, mod3, npost, npre, seq):
    rows = x2d.shape[0]
    tpb = seq // OUT_TM
    mod_spec = lambda idx: pl.BlockSpec((1, 1, D_MODEL), lambda i: ((i // tpb) * 6 + idx, 0, 0))
    vec_spec = pl.BlockSpec((1, D_MODEL), lambda i: (0, 0))
    w_bf = w_out.astype(BF16)
    return pl.pallas_call(
        _outproj_kernel,
        out_shape=(jax.ShapeDtypeStruct((rows, D_MODEL), F32),
                   jax.ShapeDtypeStruct((rows, D_MODEL), BF16)),
        grid=(rows // OUT_TM,),
        in_specs=[
            pl.BlockSpec((OUT_TM, MOBA_WIDTH), lambda i: (i, 0)),
            pl.BlockSpec((OUT_TM, NSA_WIDTH), lambda i: (i, 0)),
            pl.BlockSpec((MOBA_WIDTH, D_MODEL), lambda i: (0, 0)),
            pl.BlockSpec((NSA_WIDTH, D_MODEL), lambda i: (0, 0)),
            pl.BlockSpec((OUT_TM, D_MODEL), lambda i: (i, 0)),
            mod_spec(2), vec_spec, vec_spec, mod_spec(3), mod_spec(4),
        ],
        out_specs=(pl.BlockSpec((OUT_TM, D_MODEL), lambda i: (i, 0)),
                   pl.BlockSpec((OUT_TM, D_MODEL), lambda i: (i, 0))),
        compiler_params=_cparams(("arbitrary",)),
        name="outproj",
    )(o_moba, o_nsa, w_bf[:MOBA_WIDTH], w_bf[MOBA_WIDTH:], x2d, mod3, npost, npre, mod3, mod3)


FFN_TM = 512
FFN_TF = 512
FFN_NC = FFN_TF // MXU_DIM
FFN_RCH = 64
FFN_HALO = 16
CONV_WIDTH = 3


def _ffn_kernel(h_ref, halo_ref, w_ref, cw_ref, cb_ref, wd_ref, x1_ref, gf_ref, npost_ref, o_ref,
                haug_scr, *scr, tiles_per_seq):
    u_scrs, act_scrs = scr[:FFN_NC], scr[FFN_NC:]
    i = pl.program_id(0)
    j = pl.program_id(1)

    @pl.when(j == 0)
    def _():
        o_ref[...] = jnp.zeros(o_ref.shape, F32)
        halo = halo_ref[...]
        haug_scr[:FFN_HALO, :] = jnp.where(i % tiles_per_seq == 0, jnp.zeros_like(halo), halo)
        haug_scr[FFN_HALO:, :] = h_ref[...]

    h_aug = haug_scr[...]

    def up_proj(c):
        cols = slice(2 * c * MXU_DIM, 2 * (c + 1) * MXU_DIM)
        u_scrs[c][...] = jnp.dot(h_aug, w_ref[:, cols], preferred_element_type=F32)

    up_proj(0)
    for c in range(FFN_NC):
        if c + 1 < FFN_NC:
            up_proj(c + 1)
        u_scr, act_scr = u_scrs[c], act_scrs[c]
        cols = slice(2 * c * MXU_DIM, 2 * (c + 1) * MXU_DIM)
        cw = cw_ref[:, cols]
        cb = cb_ref[:, cols]
        for r in range(0, FFN_TM, FFN_RCH):
            out = cb
            for tap in range(CONV_WIDTH):
                ofs = FFN_HALO - (CONV_WIDTH - 1) + tap + r
                out = out + cw[tap:tap + 1, :] * u_scr[ofs:ofs + FFN_RCH, :]
            gate, val = out[:, :MXU_DIM], out[:, MXU_DIM:]
            act_scr[r:r + FFN_RCH, :] = (jax.nn.gelu(gate, approximate=True) * val).astype(BF16)
        o_ref[...] += jnp.dot(act_scr[...], wd_ref[c * MXU_DIM:(c + 1) * MXU_DIM, :],
                              preferred_element_type=F32)

    @pl.when(j == pl.num_programs(1) - 1)
    def _():
        o_ref[...] = x1_ref[...] + gf_ref[0] * _rms(o_ref[...], npost_ref[...])


def _interleave_gate_value(a):
    lead = a.shape[:-1]
    a = a.reshape(lead + (2, D_FF // MXU_DIM, MXU_DIM))
    return jnp.swapaxes(a, -3, -2).reshape(lead + (2 * D_FF,))


def _ffn(h2, x1, w_up, conv_w, conv_b, w_down, mod3, npost, seq):
    rows = h2.shape[0]
    tps = seq // FFN_TM
    nj = D_FF // FFN_TF
    w_gv = _interleave_gate_value(w_up).astype(BF16)
    cw = _interleave_gate_value(conv_w)
    cb = _interleave_gate_value(conv_b.reshape(1, 2 * D_FF))
    return pl.pallas_call(
        functools.partial(_ffn_kernel, tiles_per_seq=tps),
        out_shape=jax.ShapeDtypeStruct((rows, D_MODEL), F32),
        grid=(rows // FFN_TM, nj),
        in_specs=[
            pl.BlockSpec((FFN_TM, D_MODEL), lambda i, j: (i, 0)),
            pl.BlockSpec((FFN_HALO, D_MODEL),
                         lambda i, j: (jnp.maximum(i * (FFN_TM // FFN_HALO) - 1, 0), 0)),
            pl.BlockSpec((D_MODEL, 2 * FFN_TF), lambda i, j: (0, j)),
            pl.BlockSpec((CONV_WIDTH, 2 * FFN_TF), lambda i, j: (0, j)),
            pl.BlockSpec((1, 2 * FFN_TF), lambda i, j: (0, j)),
            pl.BlockSpec((FFN_TF, D_MODEL), lambda i, j: (j, 0)),
            pl.BlockSpec((FFN_TM, D_MODEL), lambda i, j: (i, 0), pipeline_mode=pl.Buffered(1)),
            pl.BlockSpec((1, 1, D_MODEL), lambda i, j: ((i // tps) * 6 + 5, 0, 0)),
            pl.BlockSpec((1, D_MODEL), lambda i, j: (0, 0)),
        ],
        out_specs=pl.BlockSpec((FFN_TM, D_MODEL), lambda i, j: (i, 0)),
        scratch_shapes=(
            [pltpu.VMEM((FFN_HALO + FFN_TM, D_MODEL), BF16)]
            + [pltpu.VMEM((FFN_HALO + FFN_TM, 2 * MXU_DIM), F32)] * FFN_NC
            + [pltpu.VMEM((FFN_TM, MXU_DIM), BF16)] * FFN_NC),
        compiler_params=_cparams(("arbitrary", "arbitrary")),
        name="ffn",
    )(h2, h2, w_gv, cw, cb, w_down.astype(BF16), x1, mod3, npost)


def _chunks(proj, cb, bsz, seq):
    a = proj[:, cb * HEAD_DIM:(cb + NSA_KV_HEADS) * HEAD_DIM]
    a = a.reshape(bsz, seq // CMP_STRIDE, CMP_STRIDE, NSA_KV_HEADS, HEAD_DIM).transpose(0, 3, 1, 2, 4)
    return a.reshape(bsz, NSA_KV_HEADS, seq // CMP_STRIDE, CMP_HALF_FEAT)


def kernel(x, c, positions, w_ada, b_ada, norm_pre_mix, norm_post_mix, norm_pre_ffn, norm_post_ffn,
           w_in, w_out, cmp_pos_k, cmp_w1_k, cmp_w2_k, cmp_pos_v, cmp_w1_v, cmp_w2_v,
           w_up, conv_w, conv_b, w_down):
    bsz, seq, d = x.shape
    depth = w_in.shape[0]
    assert d == D_MODEL and seq % WINDOW == 0 and seq % PROJ_TM == 0
    cos, sin = _rope_tables(positions)
    x2d = x.reshape(bsz * seq, d)
    for l in range(depth):
        mod3 = _ada(c, w_ada[l], b_ada[l]).reshape(bsz * 6, 1, d)
        assert w_in.shape[2] == PROJ_COLS + 3 * NSA_HEADS
        proj, gates = _inproj(x2d, norm_pre_mix[l].reshape(1, d), mod3, cos, sin, w_in[l], seq)
        o_moba = _moba(proj, bsz, seq)
        kc = _compress(_chunks(proj, CB_KC, bsz, seq), cmp_pos_k[l].reshape(-1, 1), cmp_w1_k[l], cmp_w2_k[l])
        vc = _compress(_chunks(proj, CB_VC, bsz, seq), cmp_pos_v[l].reshape(-1, 1), cmp_w1_v[l], cmp_w2_v[l])
        o_nsa = _nsa(proj, gates, kc, vc, bsz, seq)
        x1, h2 = _outproj(o_moba, o_nsa, w_out[l], x2d, mod3, norm_post_mix[l].reshape(1, d),
                          norm_pre_ffn[l].reshape(1, d), seq)
        x2d = _ffn(h2, x1, w_up[l], conv_w[l], conv_b[l], w_down[l], mod3,
                   norm_post_ffn[l].reshape(1, d), seq)
    return x2d.reshape(bsz, seq, d)
```

```python
import functools

import jax
import jax.numpy as jnp
from jax import lax
from jax.experimental import pallas as pl
from jax.experimental.pallas import tpu as pltpu

F32 = jnp.float32
BF16 = jnp.bfloat16

D_MODEL = 2048
HEAD_DIM = 128
HALF = HEAD_DIM // 2
MOBA_HEADS = 8
NSA_HEADS = 8
NSA_KV_HEADS = 2
NSA_GROUP = NSA_HEADS // NSA_KV_HEADS
MOBA_BLOCK = 256
MOBA_TOPK = 3
CMP_LEN = 32
CMP_STRIDE = 16
CMP_HIDDEN = 2 * HEAD_DIM
SLC_BLOCK = 64
SLC_TOPK = 16
WINDOW = 512
D_FF = 4 * D_MODEL
ROPE_THETA = 10000.0
EPS = 1e-6
NEG = -1e30
BIG = 1e9
SCALE = HEAD_DIM ** -0.5

MOBA_WIDTH = MOBA_HEADS * HEAD_DIM
NSA_WIDTH = NSA_HEADS * HEAD_DIM
KV_WIDTH = NSA_KV_HEADS * HEAD_DIM

LANES = 128
MXU_DIM = 256
VMEM_LIMIT = 56 * 1024 * 1024

CB_MQ, CB_MK, CB_MV, CB_NQ = 0, 8, 16, 24
CB_KC, CB_VC, CB_KS, CB_VS, CB_KW, CB_VW = 32, 34, 36, 38, 40, 42
PROJ_GROUPS = 44
PROJ_COLS = PROJ_GROUPS * HEAD_DIM
PROJ_COLS_PADDED = PROJ_COLS + LANES
ROPE_GROUPS = frozenset(list(range(CB_MQ, CB_MV)) + list(range(CB_NQ, CB_VC))
                        + [CB_KS, CB_KS + 1, CB_KW, CB_KW + 1])
assert PROJ_COLS % MXU_DIM == 0


def _cparams(sem):
    return pltpu.CompilerParams(dimension_semantics=sem, vmem_limit_bytes=VMEM_LIMIT)


def _dot_nt(a, b):
    return lax.dot_general(a, b, (((1,), (1,)), ((), ())), preferred_element_type=F32)


def _rms(x, w):
    ms = jnp.mean(x * x, axis=-1, keepdims=True)
    return (x * lax.rsqrt(ms + EPS)) * w


ADA_TN = 2048
ADA_TK = 512


def _ada_kernel(ct_ref, w_ref, b_ref, o_ref):
    k = pl.program_id(1)

    @pl.when(k == 0)
    def _():
        o_ref[...] = jnp.broadcast_to(b_ref[...], o_ref.shape)

    ct = ct_ref[...]
    s = ct * jax.nn.sigmoid(ct)
    w = w_ref[...]
    rows = [jnp.sum(w * s[:, b:b + 1], axis=0, keepdims=True) for b in range(ct.shape[1])]
    o_ref[...] += jnp.concatenate(rows, axis=0)


def _ada(c, w_ada, b_ada):
    bsz = c.shape[0]
    n = w_ada.shape[1]
    return pl.pallas_call(
        _ada_kernel,
        out_shape=jax.ShapeDtypeStruct((bsz, n), F32),
        grid=(n // ADA_TN, D_MODEL // ADA_TK),
        in_specs=[
            pl.BlockSpec((ADA_TK, bsz), lambda j, k: (k, 0)),
            pl.BlockSpec((ADA_TK, ADA_TN), lambda j, k: (k, j)),
            pl.BlockSpec((1, ADA_TN), lambda j, k: (0, j)),
        ],
        out_specs=pl.BlockSpec((bsz, ADA_TN), lambda j, k: (0, j)),
        compiler_params=_cparams(("arbitrary", "arbitrary")),
        name="ada",
    )(c.T, w_ada, b_ada.reshape(1, n))


ROPE_TS = 1024


def _rope_kernel(pos_ref, inv_ref, cos_ref, sin_ref):
    ang = pos_ref[...].astype(F32) * inv_ref[...]
    lane = lax.broadcasted_iota(jnp.int32, ang.shape, 1)
    cos_ref[...] = jnp.cos(ang)
    sn = jnp.sin(ang)
    sin_ref[...] = jnp.where(lane < HALF, -sn, sn)


def _rope_tables(positions):
    rows = positions.size
    inv = ROPE_THETA ** (-jnp.arange(HALF, dtype=F32) / HALF)
    inv = jnp.concatenate([inv, inv]).reshape(1, HEAD_DIM)
    return pl.pallas_call(
        _rope_kernel,
        out_shape=(jax.ShapeDtypeStruct((rows, HEAD_DIM), F32),) * 2,
        grid=(rows // ROPE_TS,),
        in_specs=[
            pl.BlockSpec((ROPE_TS, 1), lambda i: (i, 0)),
            pl.BlockSpec((1, HEAD_DIM), lambda i: (0, 0)),
        ],
        out_specs=(pl.BlockSpec((ROPE_TS, HEAD_DIM), lambda i: (i, 0)),) * 2,
        compiler_params=_cparams(("arbitrary",)),
        name="rope_tables",
    )(positions.reshape(rows, 1), inv)


PROJ_TM = 512


def _inproj_kernel(x_ref, nw_ref, sh_ref, sc_ref, cos_ref, sin_ref, w_ref, o_ref, g_ref):
    h = (_rms(x_ref[...], nw_ref[...]) * (1.0 + sc_ref[0]) + sh_ref[0]).astype(BF16)
    cos = cos_ref[...]
    sin = sin_ref[...]
    for c in range(PROJ_COLS // MXU_DIM):
        acc = jnp.dot(h, w_ref[:, c * MXU_DIM:(c + 1) * MXU_DIM], preferred_element_type=F32)
        for g in range(MXU_DIM // HEAD_DIM):
            grp = c * (MXU_DIM // HEAD_DIM) + g
            y = acc[:, g * HEAD_DIM:(g + 1) * HEAD_DIM]
            if grp in ROPE_GROUPS:
                y = y * cos + pltpu.roll(y, HALF, 1) * sin
            o_ref[:, grp * HEAD_DIM:(grp + 1) * HEAD_DIM] = y.astype(BF16)
    g_ref[...] = jnp.dot(h, w_ref[:, PROJ_COLS:], preferred_element_type=F32)


def _inproj(x2d, nw, mod3, cos, sin, w_in, seq):
    rows = x2d.shape[0]
    tpb = seq // PROJ_TM
    w_pad = jnp.pad(w_in.astype(BF16), ((0, 0), (0, PROJ_COLS_PADDED - w_in.shape[1])))
    return pl.pallas_call(
        _inproj_kernel,
        out_shape=(jax.ShapeDtypeStruct((rows, PROJ_COLS), BF16),
                   jax.ShapeDtypeStruct((rows, LANES), F32)),
        grid=(rows // PROJ_TM,),
        in_specs=[
            pl.BlockSpec((PROJ_TM, D_MODEL), lambda i: (i, 0)),
            pl.BlockSpec((1, D_MODEL), lambda i: (0, 0)),
            pl.BlockSpec((1, 1, D_MODEL), lambda i: ((i // tpb) * 6 + 0, 0, 0)),
            pl.BlockSpec((1, 1, D_MODEL), lambda i: ((i // tpb) * 6 + 1, 0, 0)),
            pl.BlockSpec((PROJ_TM, HEAD_DIM), lambda i: (i, 0)),
            pl.BlockSpec((PROJ_TM, HEAD_DIM), lambda i: (i, 0)),
            pl.BlockSpec((D_MODEL, PROJ_COLS_PADDED), lambda i: (0, 0), pipeline_mode=pl.Buffered(1)),
        ],
        out_specs=(pl.BlockSpec((PROJ_TM, PROJ_COLS), lambda i: (i, 0)),
                   pl.BlockSpec((PROJ_TM, LANES), lambda i: (i, 0))),
        compiler_params=_cparams(("arbitrary",)),
        name="inproj",
    )(x2d, nw, mod3, mod3, cos, sin, w_pad)


EXP2_SCALE = SCALE * 1.4426950408889634


def _lane_tiles(x):
    return [x[:, t * LANES:(t + 1) * LANES] for t in range(x.shape[1] // LANES)]


def _lane_fold(x, op):
    return functools.reduce(op, _lane_tiles(x))


def _ones_augment(v):
    return jnp.concatenate([v, jnp.ones(v.shape, v.dtype)], axis=1)


def _normalize(pv):
    return pv[:, :HEAD_DIM] * (1.0 / pv[:, HEAD_DIM:HEAD_DIM + 1])


def _softmax_pv(s, v_aug):
    m = jnp.max(s, axis=-1, keepdims=True)
    p = jnp.exp2((s - m) * EXP2_SCALE)
    return _normalize(jnp.dot(p.astype(BF16), v_aug, preferred_element_type=F32))


def _two_pass_attention(qk_fn, v_fn, last_mask_fn, n_full, s_scr, m_scr, acc_scr):
    m_scr[...] = jnp.full(m_scr.shape, NEG, F32)

    def pass1_tile(j):
        s = qk_fn(j)
        s_scr[j] = s
        return _lane_fold(s, jnp.maximum)

    def pass1(pair, carry):
        both = jnp.maximum(pass1_tile(2 * pair), pass1_tile(2 * pair + 1))
        m_scr[...] = jnp.maximum(m_scr[...], both)
        return carry

    lax.fori_loop(0, n_full // 2, pass1, 0)

    @pl.when(n_full % 2 == 1)
    def _():
        m_scr[...] = jnp.maximum(m_scr[...], pass1_tile(n_full - 1))

    s = last_mask_fn(qk_fn(n_full))
    s_scr[n_full] = s
    m = jnp.max(jnp.maximum(m_scr[...], _lane_fold(s, jnp.maximum)), axis=-1, keepdims=True)
    m_scr[...] = jnp.broadcast_to(m, m_scr.shape)
    acc_scr[...] = jnp.zeros(acc_scr.shape, F32)

    def pass2_tile(j):
        mb = m_scr[...]
        ps = [jnp.exp2((st - mb) * EXP2_SCALE) for st in _lane_tiles(s_scr[j])]
        p = jnp.concatenate(ps, axis=1).astype(BF16)
        return jnp.dot(p, v_fn(j), preferred_element_type=F32)

    def pass2(pair, carry):
        acc_scr[...] += pass2_tile(2 * pair) + pass2_tile(2 * pair + 1)
        return carry

    n_tiles = n_full + 1
    lax.fori_loop(0, n_tiles // 2, pass2, 0)

    @pl.when(n_tiles % 2 == 1)
    def _():
        acc_scr[...] += pass2_tile(n_tiles - 1)

    return _normalize(acc_scr[...])


def _rank_desc(score, axis, count):
    idx = lax.broadcasted_iota(jnp.int32, score.shape, axis)
    rank = jnp.zeros(score.shape, F32)
    for jp in range(count):
        other = score[:, jp:jp + 1] if axis == 1 else score[jp:jp + 1, :]
        beats = (other > score) | ((other == score) & (idx > jp))
        rank = rank + jnp.where(beats, 1.0, 0.0)
    return rank


MOBA_T = 1024
MOBA_SETUP_CH = 512


def _moba_kernel(q_ref, k_ref, v_ref, o_ref, kmean_scr, qaug_scr, kaug_scr, vaug_scr, s_scr, m_scr, acc_scr,
                 *, nblk):
    qi = pl.program_id(2)
    t = MOBA_T
    blk_shift = MOBA_BLOCK.bit_length() - 1

    @pl.when(qi == 0)
    def _():
        ch = MOBA_SETUP_CH
        kcol = lax.broadcasted_iota(jnp.int32, (ch, LANES), 1)
        krow = lax.broadcasted_iota(jnp.int32, (ch, LANES), 0)
        for n in range(nblk):
            kb = k_ref[n * MOBA_BLOCK:(n + 1) * MOBA_BLOCK, :].astype(F32)
            kmean_scr[n:n + 1, :] = jnp.mean(kb, axis=0, keepdims=True)
        kmean = kmean_scr[...].astype(BF16)
        brow = lax.broadcasted_iota(jnp.int32, (nblk, ch), 0)
        qlane = lax.broadcasted_iota(jnp.int32, (nblk, ch), 1)
        for c in range(k_ref.shape[0] // ch):
            rows = slice(c * ch, (c + 1) * ch)
            kaug_scr[rows, :HEAD_DIM] = k_ref[rows, :]
            blk = jnp.right_shift(c * ch + krow, blk_shift)
            kaug_scr[rows, HEAD_DIM:] = jnp.where(kcol == blk, 1.0, 0.0).astype(BF16)
            vaug_scr[rows, :] = _ones_augment(v_ref[rows, :])
            q = q_ref[rows, :]
            qblk = jnp.right_shift(c * ch + qlane, blk_shift)
            past = brow < qblk
            gate = jnp.where(past, _dot_nt(kmean, q), NEG)
            rank = _rank_desc(gate, 0, nblk)
            allow = (past & (rank < float(MOBA_TOPK))) | (brow == qblk)
            bias_t = jnp.where(allow, 0.0, NEG)
            bias_t = jnp.concatenate([bias_t, jnp.zeros((LANES - nblk, ch), F32)], axis=0)
            qaug_scr[rows, :HEAD_DIM] = q
            qaug_scr[rows, HEAD_DIM:] = bias_t.T.astype(BF16)

    q_aug = qaug_scr[pl.ds(pl.multiple_of(qi * t, t), t), :]

    def qk(j):
        return _dot_nt(q_aug, kaug_scr[pl.ds(pl.multiple_of(j * t, t), t), :])

    def vt(j):
        return vaug_scr[pl.ds(pl.multiple_of(j * t, t), t), :]

    def causal(s):
        r = lax.broadcasted_iota(jnp.int32, s.shape, 0)
        c = lax.broadcasted_iota(jnp.int32, s.shape, 1)
        return jnp.where(c <= r, s, NEG)

    o = _two_pass_attention(qk, vt, causal, qi, s_scr, m_scr, acc_scr)
    o_ref[...] = o.astype(BF16)


def _moba(proj, bsz, seq):
    nq = seq // MOBA_T
    nblk = seq // MOBA_BLOCK
    assert nblk % 8 == 0 and nblk <= LANES
    return pl.pallas_call(
        functools.partial(_moba_kernel, nblk=nblk),
        out_shape=jax.ShapeDtypeStruct((bsz * seq, MOBA_WIDTH), BF16),
        grid=(bsz, MOBA_HEADS, nq),
        in_specs=[
            pl.BlockSpec((seq, HEAD_DIM), lambda b, h, i: (b, CB_MQ + h)),
            pl.BlockSpec((seq, HEAD_DIM), lambda b, h, i: (b, CB_MK + h)),
            pl.BlockSpec((seq, HEAD_DIM), lambda b, h, i: (b, CB_MV + h)),
        ],
        out_specs=pl.BlockSpec((MOBA_T, HEAD_DIM), lambda b, h, i: (b * nq + i, h)),
        scratch_shapes=[
            pltpu.VMEM((nblk, HEAD_DIM), F32),
            pltpu.VMEM((seq, 2 * HEAD_DIM), BF16),
            pltpu.VMEM((seq, 2 * HEAD_DIM), BF16),
            pltpu.VMEM((seq, 2 * HEAD_DIM), BF16),
            pltpu.VMEM((nq, MOBA_T, MOBA_T), F32),
            pltpu.VMEM((MOBA_T, LANES), F32),
            pltpu.VMEM((MOBA_T, 2 * HEAD_DIM), F32),
        ],
        compiler_params=_cparams(("arbitrary", "arbitrary", "arbitrary")),
        name="moba",
    )(proj, proj, proj)


CMP_HALF_FEAT = CMP_STRIDE * HEAD_DIM


def _compress_kernel(x_ref, pos_ref, w1f_ref, w1_ref, w2_ref, o_ref):
    x = x_ref[0, 0]
    a = jnp.dot(x, w1_ref[:CMP_HALF_FEAT, :], preferred_element_type=F32)
    b = jnp.dot(x, w1_ref[CMP_HALF_FEAT:, :], preferred_element_type=F32)
    nch = x.shape[0]
    b_next = pltpu.roll(b, nch - 1, 0)
    pos_bias = jnp.sum(w1f_ref[...] * pos_ref[...], axis=0, keepdims=True)
    hid = jax.nn.gelu(a + b_next + pos_bias, approximate=True)
    o_ref[0, 0] = jnp.dot(hid.astype(BF16), w2_ref[...], preferred_element_type=F32).astype(BF16)


def _compress(xc, pos_col, w1, w2):
    bsz, hk, nch, feat = xc.shape
    return pl.pallas_call(
        _compress_kernel,
        out_shape=jax.ShapeDtypeStruct((bsz, hk, nch, HEAD_DIM), BF16),
        grid=(bsz, hk),
        in_specs=[
            pl.BlockSpec((1, 1, nch, feat), lambda b, k: (b, k, 0, 0)),
            pl.BlockSpec((CMP_LEN * HEAD_DIM, 1), lambda b, k: (0, 0)),
            pl.BlockSpec((CMP_LEN * HEAD_DIM, CMP_HIDDEN), lambda b, k: (0, 0)),
            pl.BlockSpec((CMP_LEN * HEAD_DIM, CMP_HIDDEN), lambda b, k: (0, 0)),
            pl.BlockSpec((CMP_HIDDEN, HEAD_DIM), lambda b, k: (0, 0)),
        ],
        out_specs=pl.BlockSpec((1, 1, nch, HEAD_DIM), lambda b, k: (b, k, 0, 0)),
        compiler_params=_cparams(("arbitrary", "arbitrary")),
        name="compress",
    )(xc, pos_col, w1, w1.astype(BF16), w2.astype(BF16))


NSA_TQ = 256
NSA_TK = 512
NSA_ROWS = NSA_GROUP * NSA_TQ


def _split3(x):
    hi = x.astype(BF16)
    r1 = x - hi.astype(F32)
    mid = r1.astype(BF16)
    lo = (r1 - mid.astype(F32)).astype(BF16)
    return hi, mid, lo


def _nsa_kernel(q_ref, kc_ref, vc_ref, ks_ref, vs_ref, kw_ref, vw_ref, gl_ref, mt_ref, o_ref,
                kaug_scr, vsaug_scr, vwaug_scr, s_scr, m_scr, acc_scr):
    qi = pl.program_id(2)
    tq = NSA_TQ
    tk = NSA_TK
    seq = ks_ref.shape[0]
    slc_shift = SLC_BLOCK.bit_length() - 1

    @pl.when(qi == 0)
    def _():
        kcol = lax.broadcasted_iota(jnp.int32, (tk, LANES), 1)
        krow = lax.broadcasted_iota(jnp.int32, (tk, LANES), 0)
        for c in range(seq // tk):
            rows = slice(c * tk, (c + 1) * tk)
            kaug_scr[rows, :HEAD_DIM] = ks_ref[rows, :]
            blk = jnp.right_shift(c * tk + krow, slc_shift)
            kaug_scr[rows, HEAD_DIM:] = jnp.where(kcol == blk, 1.0, 0.0).astype(BF16)
            vsaug_scr[rows, :] = _ones_augment(vs_ref[rows, :])
            vwaug_scr[rows, :] = _ones_augment(vw_ref[rows, :])

    q = q_ref[...]
    qg = jnp.concatenate([q[:, g * HEAD_DIM:(g + 1) * HEAD_DIM] for g in range(NSA_GROUP)], axis=0)
    ncmp = kc_ref.shape[2]

    def qpos_of(shape):
        return qi * tq + jnp.bitwise_and(lax.broadcasted_iota(jnp.int32, shape, 0), tq - 1)

    span = WINDOW + tq
    start = pl.multiple_of(jnp.maximum(qi * tq - WINDOW, 0), tq)
    s = _dot_nt(qg, kw_ref[pl.ds(start, span), :])
    kpos = start + lax.broadcasted_iota(jnp.int32, s.shape, 1)
    qpos = qpos_of(s.shape)
    s = jnp.where((kpos <= qpos) & (kpos > qpos - WINDOW), s, NEG)
    o_win = _softmax_pv(s, vwaug_scr[pl.ds(start, span), :])

    colc = lax.broadcasted_iota(jnp.int32, (NSA_ROWS, ncmp), 1)
    cmask = colc * CMP_STRIDE + (CMP_LEN - 1) <= qpos_of((NSA_ROWS, ncmp))
    s = jnp.where(cmask, _dot_nt(qg, kc_ref[0, 0]), NEG)
    e = jnp.exp2((s - jnp.max(s, axis=-1, keepdims=True)) * EXP2_SCALE)
    has_blk = qpos_of((NSA_ROWS, 1)) >= CMP_LEN - 1
    p = e * jnp.where(has_blk, 1.0 / jnp.sum(e, axis=-1, keepdims=True), 0.0)
    o_cmp = jnp.dot(p.astype(BF16), vc_ref[0, 0], preferred_element_type=F32)

    imp = p[0:tq]
    for g in range(1, NSA_GROUP):
        imp = imp + p[g * tq:(g + 1) * tq]
    n_slc = seq // SLC_BLOCK
    mt = mt_ref[...]
    p_slc = sum(_dot_nt(mt, part) for part in _split3(imp))
    jrow = lax.broadcasted_iota(jnp.int32, p_slc.shape, 0)
    jt = jnp.right_shift(qi * tq + lax.broadcasted_iota(jnp.int32, p_slc.shape, 1), slc_shift)
    valid = jrow <= jt
    forced = (jrow == 0) | (jrow == jt) | (jrow == jt - 1)
    score = jnp.where(valid & forced, BIG, jnp.where(valid, p_slc, NEG))
    rank = _rank_desc(score, 0, n_slc)
    sel = valid & (rank < float(min(SLC_TOPK, n_slc)))
    bias_t = jnp.where(sel, 0.0, NEG)
    bias_t = jnp.concatenate([bias_t, jnp.zeros((LANES - n_slc, tq), F32)], axis=0)
    bias = bias_t.T.astype(BF16)
    q_aug = jnp.concatenate([qg, jnp.concatenate([bias] * NSA_GROUP, axis=0)], axis=1)

    def slc_qk(j):
        return _dot_nt(q_aug, kaug_scr[pl.ds(pl.multiple_of(j * tk, tk), tk), :])

    def slc_v(j):
        return vsaug_scr[pl.ds(pl.multiple_of(j * tk, tk), tk), :]

    n_full = (qi * tq) // tk

    def slc_causal(s):
        kpos = n_full * tk + lax.broadcasted_iota(jnp.int32, s.shape, 1)
        return jnp.where(kpos <= qpos_of(s.shape), s, NEG)

    o_slc = _two_pass_attention(slc_qk, slc_v, slc_causal, n_full, s_scr, m_scr, acc_scr)

    sig = jax.nn.sigmoid(gl_ref[...])
    kv_head = pl.program_id(1)

    def gate_col(branch, g):
        cols = [sig[:, branch * NSA_HEADS + k * NSA_GROUP + g:branch * NSA_HEADS + k * NSA_GROUP + g + 1]
                for k in range(NSA_KV_HEADS)]
        col = cols[0]
        for k in range(1, NSA_KV_HEADS):
            col = jnp.where(kv_head == k, cols[k], col)
        return col

    for g in range(NSA_GROUP):
        rows = slice(g * tq, (g + 1) * tq)
        og = (gate_col(0, g) * o_cmp[rows] + gate_col(1, g) * o_slc[rows] + gate_col(2, g) * o_win[rows])
        o_ref[:, g * HEAD_DIM:(g + 1) * HEAD_DIM] = og.astype(BF16)


def _slc_weight_matrix(n_slc, ncmp):
    rs, rc = SLC_BLOCK // CMP_STRIDE, CMP_LEN // CMP_STRIDE
    j = jnp.arange(n_slc)[:, None]
    n = jnp.arange(ncmp)[None, :]
    w = jnp.zeros((n_slc, ncmp), F32)
    for o in range(rs + rc - 1):
        w_o = float(sum(1 for m in range(rs) for nn in range(rc) if m - nn + rc - 1 == o))
        w = w + jnp.where(n == rs * j + o - (rc - 1), w_o, 0.0)
    return w.astype(BF16)


def _nsa(proj, gates, kc, vc, bsz, seq):
    nq = seq // NSA_TQ
    ncmp = kc.shape[2]
    n_slc = seq // SLC_BLOCK
    assert n_slc % 8 == 0 and n_slc <= LANES and seq % NSA_TK == 0
    qw = NSA_GROUP * HEAD_DIM
    kv_spec = lambda cb: pl.BlockSpec((seq, HEAD_DIM), lambda b, k, i: (b, cb + k))
    cmp_spec = pl.BlockSpec((1, 1, ncmp, HEAD_DIM), lambda b, k, i: (b, k, 0, 0))
    return pl.pallas_call(
        _nsa_kernel,
        out_shape=jax.ShapeDtypeStruct((bsz * seq, NSA_WIDTH), BF16),
        grid=(bsz, NSA_KV_HEADS, nq),
        in_specs=[
            pl.BlockSpec((NSA_TQ, qw), lambda b, k, i: (b * nq + i, CB_NQ * HEAD_DIM // qw + k)),
            cmp_spec, cmp_spec,
            kv_spec(CB_KS), kv_spec(CB_VS), kv_spec(CB_KW), kv_spec(CB_VW),
            pl.BlockSpec((NSA_TQ, LANES), lambda b, k, i: (b * nq + i, 0)),
            pl.BlockSpec((n_slc, ncmp), lambda b, k, i: (0, 0)),
        ],
        out_specs=pl.BlockSpec((NSA_TQ, qw), lambda b, k, i: (b * nq + i, k)),
        scratch_shapes=[
            pltpu.VMEM((seq, 2 * HEAD_DIM), BF16),
            pltpu.VMEM((seq, 2 * HEAD_DIM), BF16),
            pltpu.VMEM((seq, 2 * HEAD_DIM), BF16),
            pltpu.VMEM((seq // NSA_TK, NSA_ROWS, NSA_TK), F32),
            pltpu.VMEM((NSA_ROWS, LANES), F32),
            pltpu.VMEM((NSA_ROWS, 2 * HEAD_DIM), F32),
        ],
        compiler_params=_cparams(("arbitrary", "arbitrary", "arbitrary")),
        name="nsa",
    )(proj, kc, vc, proj, proj, proj, proj, gates, _slc_weight_matrix(n_slc, ncmp))


OUT_TM = 512


def _outproj_kernel(om_ref, on_ref, wm_ref, wn_ref, x_ref, ga_ref, npost_ref, npre_ref, sh_ref, sc_ref,
                    x1_ref, h2_ref):
    y = (jnp.dot(om_ref[...], wm_ref[...], preferred_element_type=F32)
         + jnp.dot(on_ref[...], wn_ref[...], preferred_element_type=F32))
    x1 = x_ref[...] + ga_ref[0] * _rms(y, npost_ref[...])
    x1_ref[...] = x1
    h2_ref[...] = (_rms(x1, npre_ref[...]) * (1.0 + sc_ref[0]) + sh_ref[0]).astype(BF16)


def _outproj(o_moba, o_nsa, w_out, x2d, mod3, npost, npre, seq):
    rows = x2d.shape[0]
    tpb = seq // OUT_TM
    mod_spec = lambda idx: pl.BlockSpec((1, 1, D_MODEL), lambda i: ((i // tpb) * 6 + idx, 0, 0))
    vec_spec = pl.BlockSpec((1, D_MODEL), lambda i: (0, 0))
    w_bf = w_out.astype(BF16)
    return pl.pallas_call(
        _outproj_kernel,
        out_shape=(jax.ShapeDtypeStruct((rows, D_MODEL), F32),
                   jax.ShapeDtypeStruct((rows, D_MODEL), BF16)),
        grid=(rows // OUT_TM,),
        in_specs=[
            pl.BlockSpec((OUT_TM, MOBA_WIDTH), lambda i: (i, 0)),
            pl.BlockSpec((OUT_TM, NSA_WIDTH), lambda i: (i, 0)),
            pl.BlockSpec((MOBA_WIDTH, D_MODEL), lambda i: (0, 0)),
            pl.BlockSpec((NSA_WIDTH, D_MODEL), lambda i: (0, 0)),
            pl.BlockSpec((OUT_TM, D_MODEL), lambda i: (i, 0)),
            mod_spec(2), vec_spec, vec_spec, mod_spec(3), mod_spec(4),
        ],
        out_specs=(pl.BlockSpec((OUT_TM, D_MODEL), lambda i: (i, 0)),
                   pl.BlockSpec((OUT_TM, D_MODEL), lambda i: (i, 0))),
        compiler_params=_cparams(("arbitrary",)),
        name="outproj",
    )(o_moba, o_nsa, w_bf[:MOBA_WIDTH], w_bf[MOBA_WIDTH:], x2d, mod3, npost, npre, mod3, mod3)


FFN_TM = 512
FFN_TF = 512
FFN_HALO = 16
CONV_WIDTH = 3


def _ffn_kernel(h_ref, halo_ref, wg_ref, wv_ref, cwg_ref, cwv_ref, cbg_ref, cbv_ref, wd_ref,
                x1_ref, gf_ref, npost_ref, o_ref, acc_scr, u_scr, *, tiles_per_seq):
    i = pl.program_id(0)
    j = pl.program_id(1)
    halo = halo_ref[...]
    halo = jnp.where(i % tiles_per_seq == 0, jnp.zeros_like(halo), halo)
    h_aug = jnp.concatenate([halo, h_ref[...]], axis=0)

    def conv_branch(w_ref, cw_ref, cb_ref):
        u_scr[...] = jnp.dot(h_aug, w_ref[...], preferred_element_type=F32)
        cw = cw_ref[...]
        out = cb_ref[...]
        for tap in range(CONV_WIDTH):
            ofs = FFN_HALO - (CONV_WIDTH - 1) + tap
            out = out + cw[tap:tap + 1, :] * u_scr[pl.ds(ofs, FFN_TM), :]
        return out

    gate = conv_branch(wg_ref, cwg_ref, cbg_ref)
    val = conv_branch(wv_ref, cwv_ref, cbv_ref)
    act = (jax.nn.gelu(gate, approximate=True) * val).astype(BF16)
    contrib = jnp.dot(act, wd_ref[...], preferred_element_type=F32)

    @pl.when(j == 0)
    def _():
        acc_scr[...] = contrib

    @pl.when(j > 0)
    def _():
        acc_scr[...] += contrib

    @pl.when(j == pl.num_programs(1) - 1)
    def _():
        o_ref[...] = x1_ref[...] + gf_ref[0] * _rms(acc_scr[...], npost_ref[...])


def _ffn(h2, x1, w_up, conv_w, conv_b, w_down, mod3, npost, seq):
    rows = h2.shape[0]
    tps = seq // FFN_TM
    nj = D_FF // FFN_TF
    w_up_bf = w_up.astype(BF16)
    cb = conv_b.reshape(1, 2 * D_FF)
    return pl.pallas_call(
        functools.partial(_ffn_kernel, tiles_per_seq=tps),
        out_shape=jax.ShapeDtypeStruct((rows, D_MODEL), F32),
        grid=(rows // FFN_TM, nj),
        in_specs=[
            pl.BlockSpec((FFN_TM, D_MODEL), lambda i, j: (i, 0)),
            pl.BlockSpec((FFN_HALO, D_MODEL),
                         lambda i, j: (jnp.maximum(i * (FFN_TM // FFN_HALO) - 1, 0), 0)),
            pl.BlockSpec((D_MODEL, FFN_TF), lambda i, j: (0, j)),
            pl.BlockSpec((D_MODEL, FFN_TF), lambda i, j: (0, nj + j)),
            pl.BlockSpec((CONV_WIDTH, FFN_TF), lambda i, j: (0, j)),
            pl.BlockSpec((CONV_WIDTH, FFN_TF), lambda i, j: (0, nj + j)),
            pl.BlockSpec((1, FFN_TF), lambda i, j: (0, j)),
            pl.BlockSpec((1, FFN_TF), lambda i, j: (0, nj + j)),
            pl.BlockSpec((FFN_TF, D_MODEL), lambda i, j: (j, 0)),
            pl.BlockSpec((FFN_TM, D_MODEL), lambda i, j: (i, 0)),
            pl.BlockSpec((1, 1, D_MODEL), lambda i, j: ((i // tps) * 6 + 5, 0, 0)),
            pl.BlockSpec((1, D_MODEL), lambda i, j: (0, 0)),
        ],
        out_specs=pl.BlockSpec((FFN_TM, D_MODEL), lambda i, j: (i, 0)),
        scratch_shapes=[
            pltpu.VMEM((FFN_TM, D_MODEL), F32),
            pltpu.VMEM((FFN_HALO + FFN_TM, FFN_TF), F32),
        ],
        compiler_params=_cparams(("arbitrary", "arbitrary")),
        name="ffn",
    )(h2, h2, w_up_bf, w_up_bf, conv_w, conv_w, cb, cb, w_down.astype(BF16), x1, mod3, npost)


def _chunks(proj, cb, bsz, seq):
    a = proj[:, cb * HEAD_DIM:(cb + NSA_KV_HEADS) * HEAD_DIM]
    a = a.reshape(bsz, seq // CMP_STRIDE, CMP_STRIDE, NSA_KV_HEADS, HEAD_DIM).transpose(0, 3, 1, 2, 4)
    return a.reshape(bsz, NSA_KV_HEADS, seq // CMP_STRIDE, CMP_HALF_FEAT)


def kernel(x, c, positions, w_ada, b_ada, norm_pre_mix, norm_post_mix, norm_pre_ffn, norm_post_ffn,
           w_in, w_out, cmp_pos_k, cmp_w1_k, cmp_w2_k, cmp_pos_v, cmp_w1_v, cmp_w2_v,
           w_up, conv_w, conv_b, w_down):
    bsz, seq, d = x.shape
    depth = w_in.shape[0]
    assert d == D_MODEL and seq % WINDOW == 0 and seq % PROJ_TM == 0
    cos, sin = _rope_tables(positions)
    x2d = x.reshape(bsz * seq, d)
    for l in range(depth):
        mod3 = _ada(c, w_ada[l], b_ada[l]).reshape(bsz * 6, 1, d)
        assert w_in.shape[2] == PROJ_COLS + 3 * NSA_HEADS
        proj, gates = _inproj(x2d, norm_pre_mix[l].reshape(1, d), mod3, cos, sin, w_in[l], seq)
        o_moba = _moba(proj, bsz, seq)
        kc = _compress(_chunks(proj, CB_KC, bsz, seq), cmp_pos_k[l].reshape(-1, 1), cmp_w1_k[l], cmp_w2_k[l])
        vc = _compress(_chunks(proj, CB_VC, bsz, seq), cmp_pos_v[l].reshape(-1, 1), cmp_w1_v[l], cmp_w2_v[l])
        o_nsa = _nsa(proj, gates, kc, vc, bsz, seq)
        x1, h2 = _outproj(o_moba, o_nsa, w_out[l], x2d, mod3, norm_post_mix[l].reshape(1, d),
                          norm_pre_ffn[l].reshape(1, d), seq)
        x2d = _ffn(h2, x1, w_up[l], conv_w[l], conv_b[l], w_down[l], mod3,
                   norm_post_ffn[l].reshape(1, d), seq)
    return x2d.reshape(bsz, seq, d)
```

```python
import functools

import jax
import jax.numpy as jnp
from jax import lax
from jax.experimental import pallas as pl
from jax.experimental.pallas import tpu as pltpu

F32 = jnp.float32
BF16 = jnp.bfloat16

D_MODEL = 2048
HEAD_DIM = 128
HALF = HEAD_DIM // 2
MOBA_HEADS = 8
NSA_HEADS = 8
NSA_KV_HEADS = 2
NSA_GROUP = NSA_HEADS // NSA_KV_HEADS
MOBA_BLOCK = 256
MOBA_TOPK = 3
CMP_LEN = 32
CMP_STRIDE = 16
CMP_HIDDEN = 2 * HEAD_DIM
SLC_BLOCK = 64
SLC_TOPK = 16
WINDOW = 512
D_FF = 4 * D_MODEL
ROPE_THETA = 10000.0
EPS = 1e-6
NEG = -1e30
BIG = 1e9
SCALE = HEAD_DIM ** -0.5

MOBA_WIDTH = MOBA_HEADS * HEAD_DIM
NSA_WIDTH = NSA_HEADS * HEAD_DIM
KV_WIDTH = NSA_KV_HEADS * HEAD_DIM

LANES = 128
MXU_DIM = 256
VMEM_LIMIT = 56 * 1024 * 1024

CB_MQ, CB_MK, CB_MV, CB_NQ = 0, 8, 16, 24
CB_KC, CB_VC, CB_KS, CB_VS, CB_KW, CB_VW = 32, 34, 36, 38, 40, 42
PROJ_GROUPS = 44
PROJ_COLS = PROJ_GROUPS * HEAD_DIM
PROJ_COLS_PADDED = PROJ_COLS + LANES
ROPE_GROUPS = frozenset(list(range(CB_MQ, CB_MV)) + list(range(CB_NQ, CB_VC))
                        + [CB_KS, CB_KS + 1, CB_KW, CB_KW + 1])
assert PROJ_COLS % MXU_DIM == 0


def _cparams(sem):
    return pltpu.CompilerParams(dimension_semantics=sem, vmem_limit_bytes=VMEM_LIMIT)


def _dot_nt(a, b):
    return lax.dot_general(a, b, (((1,), (1,)), ((), ())), preferred_element_type=F32)


def _rms(x, w):
    ms = jnp.mean(x * x, axis=-1, keepdims=True)
    return (x * lax.rsqrt(ms + EPS)) * w


ADA_TN = 2048
ADA_TK = 1024


def _ada_kernel(ct_ref, w_ref, b_ref, o_ref):
    k = pl.program_id(1)

    @pl.when(k == 0)
    def _():
        o_ref[...] = jnp.broadcast_to(b_ref[...], o_ref.shape)

    ct = ct_ref[...]
    s = ct * jax.nn.sigmoid(ct)
    w = w_ref[...]
    rows = [jnp.sum(w * s[:, b:b + 1], axis=0, keepdims=True) for b in range(ct.shape[1])]
    o_ref[...] += jnp.concatenate(rows, axis=0)


def _ada(c, w_ada, b_ada):
    bsz = c.shape[0]
    n = w_ada.shape[1]
    return pl.pallas_call(
        _ada_kernel,
        out_shape=jax.ShapeDtypeStruct((bsz, n), F32),
        grid=(n // ADA_TN, D_MODEL // ADA_TK),
        in_specs=[
            pl.BlockSpec((ADA_TK, bsz), lambda j, k: (k, 0)),
            pl.BlockSpec((ADA_TK, ADA_TN), lambda j, k: (k, j)),
            pl.BlockSpec((1, ADA_TN), lambda j, k: (0, j)),
        ],
        out_specs=pl.BlockSpec((bsz, ADA_TN), lambda j, k: (0, j)),
        compiler_params=_cparams(("arbitrary", "arbitrary")),
        name="ada",
    )(c.T, w_ada, b_ada.reshape(1, n))


ROPE_TS = 1024


def _rope_kernel(pos_ref, inv_ref, cos_ref, sin_ref):
    half_rows = pos_ref.shape[0] // 2
    lane = lax.broadcasted_iota(jnp.int32, (half_rows, HEAD_DIM), 1)
    low = lane < HALF
    pos = jnp.where(low, pos_ref[:half_rows, :], pos_ref[half_rows:, :]).astype(F32)
    ang = pos * inv_ref[...]
    cs = jnp.cos(ang)
    sn = jnp.sin(ang)
    cs_sw = pltpu.roll(cs, HALF, 1)
    sn_sw = pltpu.roll(sn, HALF, 1)
    cos_ref[:half_rows, :] = jnp.where(low, cs, cs_sw)
    cos_ref[half_rows:, :] = jnp.where(low, cs_sw, cs)
    sin_ref[:half_rows, :] = jnp.where(low, -sn, sn_sw)
    sin_ref[half_rows:, :] = jnp.where(low, -sn_sw, sn)


def _rope_tables(positions):
    rows = positions.size
    inv = ROPE_THETA ** (-jnp.arange(HALF, dtype=F32) / HALF)
    inv = jnp.concatenate([inv, inv]).reshape(1, HEAD_DIM)
    return pl.pallas_call(
        _rope_kernel,
        out_shape=(jax.ShapeDtypeStruct((rows, HEAD_DIM), F32),) * 2,
        grid=(rows // ROPE_TS,),
        in_specs=[
            pl.BlockSpec((ROPE_TS, 1), lambda i: (i, 0)),
            pl.BlockSpec((1, HEAD_DIM), lambda i: (0, 0)),
        ],
        out_specs=(pl.BlockSpec((ROPE_TS, HEAD_DIM), lambda i: (i, 0)),) * 2,
        compiler_params=_cparams(("arbitrary",)),
        name="rope_tables",
    )(positions.reshape(rows, 1), inv)


PROJ_TM = 512


def _inproj_kernel(x_ref, nw_ref, sh_ref, sc_ref, cos_ref, sin_ref, w_ref, o_ref, g_ref):
    h = (_rms(x_ref[...], nw_ref[...]) * (1.0 + sc_ref[0]) + sh_ref[0]).astype(BF16)
    cos = cos_ref[...]
    sin = sin_ref[...]
    for c in range(PROJ_COLS // MXU_DIM):
        acc = jnp.dot(h, w_ref[:, c * MXU_DIM:(c + 1) * MXU_DIM], preferred_element_type=F32)
        for g in range(MXU_DIM // HEAD_DIM):
            grp = c * (MXU_DIM // HEAD_DIM) + g
            y = acc[:, g * HEAD_DIM:(g + 1) * HEAD_DIM]
            if grp in ROPE_GROUPS:
                y = y * cos + pltpu.roll(y, HALF, 1) * sin
            o_ref[:, grp * HEAD_DIM:(grp + 1) * HEAD_DIM] = y.astype(BF16)
    g_ref[...] = jnp.dot(h, w_ref[:, PROJ_COLS:], preferred_element_type=F32)


def _inproj(x2d, nw, mod3, cos, sin, w_in, seq):
    rows = x2d.shape[0]
    tpb = seq // PROJ_TM
    w_pad = jnp.pad(w_in.astype(BF16), ((0, 0), (0, PROJ_COLS_PADDED - w_in.shape[1])))
    return pl.pallas_call(
        _inproj_kernel,
        out_shape=(jax.ShapeDtypeStruct((rows, PROJ_COLS), BF16),
                   jax.ShapeDtypeStruct((rows, LANES), F32)),
        grid=(rows // PROJ_TM,),
        in_specs=[
            pl.BlockSpec((PROJ_TM, D_MODEL), lambda i: (i, 0)),
            pl.BlockSpec((1, D_MODEL), lambda i: (0, 0)),
            pl.BlockSpec((1, 1, D_MODEL), lambda i: ((i // tpb) * 6 + 0, 0, 0)),
            pl.BlockSpec((1, 1, D_MODEL), lambda i: ((i // tpb) * 6 + 1, 0, 0)),
            pl.BlockSpec((PROJ_TM, HEAD_DIM), lambda i: (i, 0)),
            pl.BlockSpec((PROJ_TM, HEAD_DIM), lambda i: (i, 0)),
            pl.BlockSpec((D_MODEL, PROJ_COLS_PADDED), lambda i: (0, 0), pipeline_mode=pl.Buffered(1)),
        ],
        out_specs=(pl.BlockSpec((PROJ_TM, PROJ_COLS), lambda i: (i, 0)),
                   pl.BlockSpec((PROJ_TM, LANES), lambda i: (i, 0))),
        compiler_params=_cparams(("arbitrary",)),
        name="inproj",
    )(x2d, nw, mod3, mod3, cos, sin, w_pad)


EXP2_SCALE = SCALE * 1.4426950408889634


def _lane_tiles(x):
    return [x[:, t * LANES:(t + 1) * LANES] for t in range(x.shape[1] // LANES)]


def _lane_fold(x, op):
    return functools.reduce(op, _lane_tiles(x))


def _ones_augment(v):
    return jnp.concatenate([v, jnp.ones(v.shape, v.dtype)], axis=1)


def _normalize(pv):
    return pv[:, :HEAD_DIM] * (1.0 / pv[:, HEAD_DIM:HEAD_DIM + 1])


def _softmax_pv(s, v_aug):
    m = jnp.max(s, axis=-1, keepdims=True)
    p = jnp.exp2((s - m) * EXP2_SCALE)
    return _normalize(jnp.dot(p.astype(BF16), v_aug, preferred_element_type=F32))


def _two_pass_attention(qk_fn, v_fn, last_mask_fn, n_full, s_scr, m_scr, acc_scr):
    m_scr[...] = jnp.full(m_scr.shape, NEG, F32)

    def pass1_tile(j):
        s = qk_fn(j)
        s_scr[j] = s
        return _lane_fold(s, jnp.maximum)

    def pass1(pair, carry):
        both = jnp.maximum(pass1_tile(2 * pair), pass1_tile(2 * pair + 1))
        m_scr[...] = jnp.maximum(m_scr[...], both)
        return carry

    lax.fori_loop(0, n_full // 2, pass1, 0)

    @pl.when(n_full % 2 == 1)
    def _():
        m_scr[...] = jnp.maximum(m_scr[...], pass1_tile(n_full - 1))

    s = last_mask_fn(qk_fn(n_full))
    s_scr[n_full] = s
    m = jnp.max(jnp.maximum(m_scr[...], _lane_fold(s, jnp.maximum)), axis=-1, keepdims=True)
    m_scr[...] = jnp.broadcast_to(m, m_scr.shape)
    acc_scr[...] = jnp.zeros(acc_scr.shape, F32)

    def pass2_tile(j):
        mb = m_scr[...]
        ps = [jnp.exp2((st - mb) * EXP2_SCALE) for st in _lane_tiles(s_scr[j])]
        p = jnp.concatenate(ps, axis=1).astype(BF16)
        return jnp.dot(p, v_fn(j), preferred_element_type=F32)

    def pass2(pair, carry):
        acc_scr[...] += pass2_tile(2 * pair) + pass2_tile(2 * pair + 1)
        return carry

    n_tiles = n_full + 1
    lax.fori_loop(0, n_tiles // 2, pass2, 0)

    @pl.when(n_tiles % 2 == 1)
    def _():
        acc_scr[...] += pass2_tile(n_tiles - 1)

    return _normalize(acc_scr[...])


def _rank_desc(score, axis, count):
    idx = lax.broadcasted_iota(jnp.int32, score.shape, axis)
    rank = jnp.zeros(score.shape, F32)
    for jp in range(count):
        other = score[:, jp:jp + 1] if axis == 1 else score[jp:jp + 1, :]
        beats = (other > score) | ((other == score) & (idx > jp))
        rank = rank + jnp.where(beats, 1.0, 0.0)
    return rank


MOBA_T = 1024
MOBA_SETUP_CH = 512


def _moba_kernel(q_ref, k_ref, v_ref, o_ref, kmean_scr, qaug_scr, kaug_scr, vaug_scr, s_scr, m_scr, acc_scr,
                 *, nblk):
    qi = pl.program_id(2)
    t = MOBA_T
    blk_shift = MOBA_BLOCK.bit_length() - 1

    @pl.when(qi == 0)
    def _():
        ch = MOBA_SETUP_CH
        kcol = lax.broadcasted_iota(jnp.int32, (ch, LANES), 1)
        krow = lax.broadcasted_iota(jnp.int32, (ch, LANES), 0)
        for n in range(nblk):
            kb = k_ref[n * MOBA_BLOCK:(n + 1) * MOBA_BLOCK, :].astype(F32)
            kmean_scr[n:n + 1, :] = jnp.mean(kb, axis=0, keepdims=True)
        kmean = kmean_scr[...].astype(BF16)
        brow = lax.broadcasted_iota(jnp.int32, (nblk, ch), 0)
        qlane = lax.broadcasted_iota(jnp.int32, (nblk, ch), 1)
        for c in range(k_ref.shape[0] // ch):
            rows = slice(c * ch, (c + 1) * ch)
            kaug_scr[rows, :HEAD_DIM] = k_ref[rows, :]
            blk = jnp.right_shift(c * ch + krow, blk_shift)
            kaug_scr[rows, HEAD_DIM:] = jnp.where(kcol == blk, 1.0, 0.0).astype(BF16)
            vaug_scr[rows, :] = _ones_augment(v_ref[rows, :])
            q = q_ref[rows, :]
            qblk = jnp.right_shift(c * ch + qlane, blk_shift)
            past = brow < qblk
            gate = jnp.where(past, _dot_nt(kmean, q), NEG)
            rank = _rank_desc(gate, 0, nblk)
            allow = (past & (rank < float(MOBA_TOPK))) | (brow == qblk)
            bias_t = jnp.where(allow, 0.0, NEG)
            bias_t = jnp.concatenate([bias_t, jnp.zeros((LANES - nblk, ch), F32)], axis=0)
            qaug_scr[rows, :HEAD_DIM] = q
            qaug_scr[rows, HEAD_DIM:] = bias_t.T.astype(BF16)

    q_aug = qaug_scr[pl.ds(pl.multiple_of(qi * t, t), t), :]

    def qk(j):
        return _dot_nt(q_aug, kaug_scr[pl.ds(pl.multiple_of(j * t, t), t), :])

    def vt(j):
        return vaug_scr[pl.ds(pl.multiple_of(j * t, t), t), :]

    def causal(s):
        r = lax.broadcasted_iota(jnp.int32, s.shape, 0)
        c = lax.broadcasted_iota(jnp.int32, s.shape, 1)
        return jnp.where(c <= r, s, NEG)

    o = _two_pass_attention(qk, vt, causal, qi, s_scr, m_scr, acc_scr)
    o_ref[...] = o.astype(BF16)


def _moba(proj, bsz, seq):
    nq = seq // MOBA_T
    nblk = seq // MOBA_BLOCK
    assert nblk % 8 == 0 and nblk <= LANES
    return pl.pallas_call(
        functools.partial(_moba_kernel, nblk=nblk),
        out_shape=jax.ShapeDtypeStruct((bsz * seq, MOBA_WIDTH), BF16),
        grid=(bsz, MOBA_HEADS, nq),
        in_specs=[
            pl.BlockSpec((seq, HEAD_DIM), lambda b, h, i: (b, CB_MQ + h)),
            pl.BlockSpec((seq, HEAD_DIM), lambda b, h, i: (b, CB_MK + h)),
            pl.BlockSpec((seq, HEAD_DIM), lambda b, h, i: (b, CB_MV + h)),
        ],
        out_specs=pl.BlockSpec((MOBA_T, HEAD_DIM), lambda b, h, i: (b * nq + i, h)),
        scratch_shapes=[
            pltpu.VMEM((nblk, HEAD_DIM), F32),
            pltpu.VMEM((seq, 2 * HEAD_DIM), BF16),
            pltpu.VMEM((seq, 2 * HEAD_DIM), BF16),
            pltpu.VMEM((seq, 2 * HEAD_DIM), BF16),
            pltpu.VMEM((nq, MOBA_T, MOBA_T), F32),
            pltpu.VMEM((MOBA_T, LANES), F32),
            pltpu.VMEM((MOBA_T, 2 * HEAD_DIM), F32),
        ],
        compiler_params=_cparams(("arbitrary", "arbitrary", "arbitrary")),
        name="moba",
    )(proj, proj, proj)


CMP_HALF_FEAT = CMP_STRIDE * HEAD_DIM


def _compress_kernel(x_ref, pos_ref, w1f_ref, w1_ref, w2_ref, o_ref):
    x = x_ref[0, 0]
    a = jnp.dot(x, w1_ref[:CMP_HALF_FEAT, :], preferred_element_type=F32)
    b = jnp.dot(x, w1_ref[CMP_HALF_FEAT:, :], preferred_element_type=F32)
    nch = x.shape[0]
    b_next = pltpu.roll(b, nch - 1, 0)
    pos_bias = jnp.sum(w1f_ref[...] * pos_ref[...], axis=0, keepdims=True)
    hid = jax.nn.gelu(a + b_next + pos_bias, approximate=True)
    o_ref[0, 0] = jnp.dot(hid.astype(BF16), w2_ref[...], preferred_element_type=F32).astype(BF16)


def _compress(xc, pos_col, w1, w2):
    bsz, hk, nch, feat = xc.shape
    return pl.pallas_call(
        _compress_kernel,
        out_shape=jax.ShapeDtypeStruct((bsz, hk, nch, HEAD_DIM), BF16),
        grid=(bsz, hk),
        in_specs=[
            pl.BlockSpec((1, 1, nch, feat), lambda b, k: (b, k, 0, 0)),
            pl.BlockSpec((CMP_LEN * HEAD_DIM, 1), lambda b, k: (0, 0)),
            pl.BlockSpec((CMP_LEN * HEAD_DIM, CMP_HIDDEN), lambda b, k: (0, 0)),
            pl.BlockSpec((CMP_LEN * HEAD_DIM, CMP_HIDDEN), lambda b, k: (0, 0)),
            pl.BlockSpec((CMP_HIDDEN, HEAD_DIM), lambda b, k: (0, 0)),
        ],
        out_specs=pl.BlockSpec((1, 1, nch, HEAD_DIM), lambda b, k: (b, k, 0, 0)),
        compiler_params=_cparams(("arbitrary", "arbitrary")),
        name="compress",
    )(xc, pos_col, w1, w1.astype(BF16), w2.astype(BF16))


NSA_TQ = 256
NSA_TK = 512
NSA_ROWS = NSA_GROUP * NSA_TQ


def _split3(x):
    hi = x.astype(BF16)
    r1 = x - hi.astype(F32)
    mid = r1.astype(BF16)
    lo = (r1 - mid.astype(F32)).astype(BF16)
    return hi, mid, lo


def _nsa_kernel(q_ref, kc_ref, vc_ref, ks_ref, vs_ref, kw_ref, vw_ref, gl_ref, mt_ref, wb_ref, o_ref,
                kaug_scr, vsaug_scr, vwaug_scr, s_scr, m_scr, acc_scr):
    qi = pl.program_id(2)
    tq = NSA_TQ
    tk = NSA_TK
    seq = ks_ref.shape[0]
    slc_shift = SLC_BLOCK.bit_length() - 1

    @pl.when(qi == 0)
    def _():
        kcol = lax.broadcasted_iota(jnp.int32, (tk, LANES), 1)
        krow = lax.broadcasted_iota(jnp.int32, (tk, LANES), 0)
        for c in range(seq // tk):
            rows = slice(c * tk, (c + 1) * tk)
            kaug_scr[rows, :HEAD_DIM] = ks_ref[rows, :]
            blk = jnp.right_shift(c * tk + krow, slc_shift)
            kaug_scr[rows, HEAD_DIM:] = jnp.where(kcol == blk, 1.0, 0.0).astype(BF16)
            vsaug_scr[rows, :] = _ones_augment(vs_ref[rows, :])
            vwaug_scr[rows, :] = _ones_augment(vw_ref[rows, :])

    q = q_ref[...]
    qg = jnp.concatenate([q[:, g * HEAD_DIM:(g + 1) * HEAD_DIM] for g in range(NSA_GROUP)], axis=0)
    ncmp = kc_ref.shape[2]

    def qpos_of(shape):
        return qi * tq + jnp.bitwise_and(lax.broadcasted_iota(jnp.int32, shape, 0), tq - 1)

    span = WINDOW + tq
    start = pl.multiple_of(jnp.maximum(qi * tq - WINDOW, 0), tq)
    s = _dot_nt(qg, kw_ref[pl.ds(start, span), :])
    s = (s.reshape(NSA_GROUP, tq, span) + wb_ref[...]).reshape(NSA_ROWS, span)
    o_win = _softmax_pv(s, vwaug_scr[pl.ds(start, span), :])

    colc = lax.broadcasted_iota(jnp.int32, (NSA_ROWS, ncmp), 1)
    cmask = colc * CMP_STRIDE + (CMP_LEN - 1) <= qpos_of((NSA_ROWS, ncmp))
    s = jnp.where(cmask, _dot_nt(qg, kc_ref[0, 0]), NEG)
    e = jnp.exp2((s - jnp.max(s, axis=-1, keepdims=True)) * EXP2_SCALE)
    has_blk = qpos_of((NSA_ROWS, 1)) >= CMP_LEN - 1
    p = e * jnp.where(has_blk, 1.0 / jnp.sum(e, axis=-1, keepdims=True), 0.0)
    o_cmp = jnp.dot(p.astype(BF16), vc_ref[0, 0], preferred_element_type=F32)

    imp = p[0:tq]
    for g in range(1, NSA_GROUP):
        imp = imp + p[g * tq:(g + 1) * tq]
    n_slc = seq // SLC_BLOCK
    mt = mt_ref[...]
    p_slc = sum(_dot_nt(mt, part) for part in _split3(imp))
    jrow = lax.broadcasted_iota(jnp.int32, p_slc.shape, 0)
    jt = jnp.right_shift(qi * tq + lax.broadcasted_iota(jnp.int32, p_slc.shape, 1), slc_shift)
    valid = jrow <= jt
    forced = (jrow == 0) | (jrow == jt) | (jrow == jt - 1)
    score = jnp.where(valid & forced, BIG, jnp.where(valid, p_slc, NEG))
    rank = _rank_desc(score, 0, n_slc)
    sel = valid & (rank < float(min(SLC_TOPK, n_slc)))
    bias_t = jnp.where(sel, 0.0, NEG)
    bias_t = jnp.concatenate([bias_t, jnp.zeros((LANES - n_slc, tq), F32)], axis=0)
    bias = bias_t.T.astype(BF16)
    q_aug = jnp.concatenate([qg, jnp.concatenate([bias] * NSA_GROUP, axis=0)], axis=1)

    def slc_qk(j):
        return _dot_nt(q_aug, kaug_scr[pl.ds(pl.multiple_of(j * tk, tk), tk), :])

    def slc_v(j):
        return vsaug_scr[pl.ds(pl.multiple_of(j * tk, tk), tk), :]

    n_full = (qi * tq) // tk

    def slc_causal(s):
        kpos = n_full * tk + lax.broadcasted_iota(jnp.int32, s.shape, 1)
        return jnp.where(kpos <= qpos_of(s.shape), s, NEG)

    o_slc = _two_pass_attention(slc_qk, slc_v, slc_causal, n_full, s_scr, m_scr, acc_scr)

    sig = jax.nn.sigmoid(gl_ref[...])
    kv_head = pl.program_id(1)

    def gate_col(branch, g):
        cols = [sig[:, branch * NSA_HEADS + k * NSA_GROUP + g:branch * NSA_HEADS + k * NSA_GROUP + g + 1]
                for k in range(NSA_KV_HEADS)]
        col = cols[0]
        for k in range(1, NSA_KV_HEADS):
            col = jnp.where(kv_head == k, cols[k], col)
        return col

    for g in range(NSA_GROUP):
        rows = slice(g * tq, (g + 1) * tq)
        og = (gate_col(0, g) * o_cmp[rows] + gate_col(1, g) * o_slc[rows] + gate_col(2, g) * o_win[rows])
        o_ref[:, g * HEAD_DIM:(g + 1) * HEAD_DIM] = og.astype(BF16)


def _slc_weight_matrix(n_slc, ncmp):
    rs, rc = SLC_BLOCK // CMP_STRIDE, CMP_LEN // CMP_STRIDE
    j = jnp.arange(n_slc)[:, None]
    n = jnp.arange(ncmp)[None, :]
    w = jnp.zeros((n_slc, ncmp), F32)
    for o in range(rs + rc - 1):
        w_o = float(sum(1 for m in range(rs) for nn in range(rc) if m - nn + rc - 1 == o))
        w = w + jnp.where(n == rs * j + o - (rc - 1), w_o, 0.0)
    return w.astype(BF16)


def _window_bias():
    span = WINDOW + NSA_TQ
    cases = WINDOW // NSA_TQ + 1
    c = jnp.arange(cases)[:, None, None]
    qpos = c * NSA_TQ + jnp.arange(NSA_TQ)[None, :, None]
    kpos = jnp.maximum(c * NSA_TQ - WINDOW, 0) + jnp.arange(span)[None, None, :]
    return jnp.where((kpos <= qpos) & (kpos > qpos - WINDOW), 0.0, NEG).astype(F32)


def _nsa(proj, gates, kc, vc, bsz, seq):
    nq = seq // NSA_TQ
    ncmp = kc.shape[2]
    n_slc = seq // SLC_BLOCK
    assert n_slc % 8 == 0 and n_slc <= LANES and seq % NSA_TK == 0
    wbias = _window_bias()
    last_case = wbias.shape[0] - 1
    qw = NSA_GROUP * HEAD_DIM
    kv_spec = lambda cb: pl.BlockSpec((seq, HEAD_DIM), lambda b, k, i: (b, cb + k))
    cmp_spec = pl.BlockSpec((1, 1, ncmp, HEAD_DIM), lambda b, k, i: (b, k, 0, 0))
    return pl.pallas_call(
        _nsa_kernel,
        out_shape=jax.ShapeDtypeStruct((bsz * seq, NSA_WIDTH), BF16),
        grid=(bsz, NSA_KV_HEADS, nq),
        in_specs=[
            pl.BlockSpec((NSA_TQ, qw), lambda b, k, i: (b * nq + i, CB_NQ * HEAD_DIM // qw + k)),
            cmp_spec, cmp_spec,
            kv_spec(CB_KS), kv_spec(CB_VS), kv_spec(CB_KW), kv_spec(CB_VW),
            pl.BlockSpec((NSA_TQ, LANES), lambda b, k, i: (b * nq + i, 0)),
            pl.BlockSpec((n_slc, ncmp), lambda b, k, i: (0, 0)),
            pl.BlockSpec((1,) + wbias.shape[1:], lambda b, k, i: (jnp.minimum(i, last_case), 0, 0)),
        ],
        out_specs=pl.BlockSpec((NSA_TQ, qw), lambda b, k, i: (b * nq + i, k)),
        scratch_shapes=[
            pltpu.VMEM((seq, 2 * HEAD_DIM), BF16),
            pltpu.VMEM((seq, 2 * HEAD_DIM), BF16),
            pltpu.VMEM((seq, 2 * HEAD_DIM), BF16),
            pltpu.VMEM((seq // NSA_TK, NSA_ROWS, NSA_TK), F32),
            pltpu.VMEM((NSA_ROWS, LANES), F32),
            pltpu.VMEM((NSA_ROWS, 2 * HEAD_DIM), F32),
        ],
        compiler_params=_cparams(("arbitrary", "arbitrary", "arbitrary")),
        name="nsa",
    )(proj, kc, vc, proj, proj, proj, proj, gates, _slc_weight_matrix(n_slc, ncmp), wbias)


OUT_TM = 512


def _outproj_kernel(om_ref, on_ref, wm_ref, wn_ref, x_ref, ga_ref, npost_ref, npre_ref, sh_ref, sc_ref,
                    x1_ref, h2_ref):
    y = (jnp.dot(om_ref[...], wm_ref[...], preferred_element_type=F32)
         + jnp.dot(on_ref[...], wn_ref[...], preferred_element_type=F32))
    x1 = x_ref[...] + ga_ref[0] * _rms(y, npost_ref[...])
    x1_ref[...] = x1
    h2_ref[...] = (_rms(x1, npre_ref[...]) * (1.0 + sc_ref[0]) + sh_ref[0]).astype(BF16)


def _outproj(o_moba, o_nsa, w_out, x2d, mod3, npost, npre, seq):
    rows = x2d.shape[0]
    tpb = seq // OUT_TM
    mod_spec = lambda idx: pl.BlockSpec((1, 1, D_MODEL), lambda i: ((i // tpb) * 6 + idx, 0, 0))
    vec_spec = pl.BlockSpec((1, D_MODEL), lambda i: (0, 0))
    w_bf = w_out.astype(BF16)
    return pl.pallas_call(
        _outproj_kernel,
        out_shape=(jax.ShapeDtypeStruct((rows, D_MODEL), F32),
                   jax.ShapeDtypeStruct((rows, D_MODEL), BF16)),
        grid=(rows // OUT_TM,),
        in_specs=[
            pl.BlockSpec((OUT_TM, MOBA_WIDTH), lambda i: (i, 0)),
            pl.BlockSpec((OUT_TM, NSA_WIDTH), lambda i: (i, 0)),
            pl.BlockSpec((MOBA_WIDTH, D_MODEL), lambda i: (0, 0)),
            pl.BlockSpec((NSA_WIDTH, D_MODEL), lambda i: (0, 0)),
            pl.BlockSpec((OUT_TM, D_MODEL), lambda i: (i, 0)),
            mod_spec(2), vec_spec, vec_spec, mod_spec(3), mod_spec(4),
        ],
        out_specs=(pl.BlockSpec((OUT_TM, D_MODEL), lambda i: (i, 0)),
                   pl.BlockSpec((OUT_TM, D_MODEL), lambda i: (i, 0))),
        compiler_params=_cparams(("arbitrary",)),
        name="outproj",
    )(o_moba, o_nsa, w_bf[:MOBA_WIDTH], w_bf[MOBA_WIDTH:], x2d, mod3, npost, npre, mod3, mod3)


FFN_TM = 512
FFN_TF = 512
FFN_HALO = 16
CONV_WIDTH = 3


def _ffn_kernel(h_ref, halo_ref, wg_ref, wv_ref, cwg_ref, cwv_ref, cbg_ref, cbv_ref, wd_ref,
                x1_ref, gf_ref, npost_ref, o_ref, acc_scr, ug_scr, uv_scr, *, tiles_per_seq):
    i = pl.program_id(0)
    j = pl.program_id(1)
    halo = halo_ref[...]
    halo = jnp.where(i % tiles_per_seq == 0, jnp.zeros_like(halo), halo)
    h_aug = jnp.concatenate([halo, h_ref[...]], axis=0)

    def conv(u_scr, cw_ref, cb_ref):
        cw = cw_ref[...]
        out = cb_ref[...]
        for tap in range(CONV_WIDTH):
            ofs = FFN_HALO - (CONV_WIDTH - 1) + tap
            out = out + cw[tap:tap + 1, :] * u_scr[pl.ds(ofs, FFN_TM), :]
        return out

    ug_scr[...] = jnp.dot(h_aug, wg_ref[...], preferred_element_type=F32)
    uv_scr[...] = jnp.dot(h_aug, wv_ref[...], preferred_element_type=F32)
    gate = conv(ug_scr, cwg_ref, cbg_ref)
    val = conv(uv_scr, cwv_ref, cbv_ref)
    act = (jax.nn.gelu(gate, approximate=True) * val).astype(BF16)
    contrib = jnp.dot(act, wd_ref[...], preferred_element_type=F32)

    @pl.when(j == 0)
    def _():
        acc_scr[...] = contrib

    @pl.when(j > 0)
    def _():
        acc_scr[...] += contrib

    @pl.when(j == pl.num_programs(1) - 1)
    def _():
        o_ref[...] = x1_ref[...] + gf_ref[0] * _rms(acc_scr[...], npost_ref[...])


def _ffn(h2, x1, w_up, conv_w, conv_b, w_down, mod3, npost, seq):
    rows = h2.shape[0]
    tps = seq // FFN_TM
    nj = D_FF // FFN_TF
    w_up_bf = w_up.astype(BF16)
    cb = conv_b.reshape(1, 2 * D_FF)
    return pl.pallas_call(
        functools.partial(_ffn_kernel, tiles_per_seq=tps),
        out_shape=jax.ShapeDtypeStruct((rows, D_MODEL), F32),
        grid=(rows // FFN_TM, nj),
        in_specs=[
            pl.BlockSpec((FFN_TM, D_MODEL), lambda i, j: (i, 0)),
            pl.BlockSpec((FFN_HALO, D_MODEL),
                         lambda i, j: (jnp.maximum(i * (FFN_TM // FFN_HALO) - 1, 0), 0)),
            pl.BlockSpec((D_MODEL, FFN_TF), lambda i, j: (0, j)),
            pl.BlockSpec((D_MODEL, FFN_TF), lambda i, j: (0, nj + j)),
            pl.BlockSpec((CONV_WIDTH, FFN_TF), lambda i, j: (0, j)),
            pl.BlockSpec((CONV_WIDTH, FFN_TF), lambda i, j: (0, nj + j)),
            pl.BlockSpec((1, FFN_TF), lambda i, j: (0, j)),
            pl.BlockSpec((1, FFN_TF), lambda i, j: (0, nj + j)),
            pl.BlockSpec((FFN_TF, D_MODEL), lambda i, j: (j, 0)),
            pl.BlockSpec((FFN_TM, D_MODEL), lambda i, j: (i, 0), pipeline_mode=pl.Buffered(1)),
            pl.BlockSpec((1, 1, D_MODEL), lambda i, j: ((i // tps) * 6 + 5, 0, 0)),
            pl.BlockSpec((1, D_MODEL), lambda i, j: (0, 0)),
        ],
        out_specs=pl.BlockSpec((FFN_TM, D_MODEL), lambda i, j: (i, 0)),
        scratch_shapes=[
            pltpu.VMEM((FFN_TM, D_MODEL), F32),
            pltpu.VMEM((FFN_HALO + FFN_TM, FFN_TF), F32),
            pltpu.VMEM((FFN_HALO + FFN_TM, FFN_TF), F32),
        ],
        compiler_params=_cparams(("arbitrary", "arbitrary")),
        name="ffn",
    )(h2, h2, w_up_bf, w_up_bf, conv_w, conv_w, cb, cb, w_down.astype(BF16), x1, mod3, npost)


def _chunks(proj, cb, bsz, seq):
    a = proj[:, cb * HEAD_DIM:(cb + NSA_KV_HEADS) * HEAD_DIM]
    a = a.reshape(bsz, seq // CMP_STRIDE, CMP_STRIDE, NSA_KV_HEADS, HEAD_DIM).transpose(0, 3, 1, 2, 4)
    return a.reshape(bsz, NSA_KV_HEADS, seq // CMP_STRIDE, CMP_HALF_FEAT)


def kernel(x, c, positions, w_ada, b_ada, norm_pre_mix, norm_post_mix, norm_pre_ffn, norm_post_ffn,
           w_in, w_out, cmp_pos_k, cmp_w1_k, cmp_w2_k, cmp_pos_v, cmp_w1_v, cmp_w2_v,
           w_up, conv_w, conv_b, w_down):
    bsz, seq, d = x.shape
    depth = w_in.shape[0]
    assert d == D_MODEL and seq % WINDOW == 0 and seq % PROJ_TM == 0
    cos, sin = _rope_tables(positions)
    x2d = x.reshape(bsz * seq, d)
    for l in range(depth):
        mod3 = _ada(c, w_ada[l], b_ada[l]).reshape(bsz * 6, 1, d)
        assert w_in.shape[2] == PROJ_COLS + 3 * NSA_HEADS
        proj, gates = _inproj(x2d, norm_pre_mix[l].reshape(1, d), mod3, cos, sin, w_in[l], seq)
        o_moba = _moba(proj, bsz, seq)
        kc = _compress(_chunks(proj, CB_KC, bsz, seq), cmp_pos_k[l].reshape(-1, 1), cmp_w1_k[l], cmp_w2_k[l])
        vc = _compress(_chunks(proj, CB_VC, bsz, seq), cmp_pos_v[l].reshape(-1, 1), cmp_w1_v[l], cmp_w2_v[l])
        o_nsa = _nsa(proj, gates, kc, vc, bsz, seq)
        x1, h2 = _outproj(o_moba, o_nsa, w_out[l], x2d, mod3, norm_post_mix[l].reshape(1, d),
                          norm_pre_ffn[l].reshape(1, d), seq)
        x2d = _ffn(h2, x1, w_up[l], conv_w[l], conv_b[l], w_down[l], mod3,
                   norm_post_ffn[l].reshape(1, d), seq)
    return x2d.reshape(bsz, seq, d)
```

```python
import functools

import jax
import jax.numpy as jnp
from jax import lax
from jax.experimental import pallas as pl
from jax.experimental.pallas import tpu as pltpu

F32 = jnp.float32
BF16 = jnp.bfloat16

D_MODEL = 2048
HEAD_DIM = 128
HALF = HEAD_DIM // 2
MOBA_HEADS = 8
NSA_HEADS = 8
NSA_KV_HEADS = 2
NSA_GROUP = NSA_HEADS // NSA_KV_HEADS
MOBA_BLOCK = 256
MOBA_TOPK = 3
CMP_LEN = 32
CMP_STRIDE = 16
CMP_HIDDEN = 2 * HEAD_DIM
SLC_BLOCK = 64
SLC_TOPK = 16
WINDOW = 512
D_FF = 4 * D_MODEL
ROPE_THETA = 10000.0
EPS = 1e-6
NEG = -1e30
BIG = 1e9
SCALE = HEAD_DIM ** -0.5

MOBA_WIDTH = MOBA_HEADS * HEAD_DIM
NSA_WIDTH = NSA_HEADS * HEAD_DIM
KV_WIDTH = NSA_KV_HEADS * HEAD_DIM

LANES = 128
MXU_DIM = 256
VMEM_LIMIT = 56 * 1024 * 1024

CB_MQ, CB_MK, CB_MV, CB_NQ = 0, 8, 16, 24
CB_KC, CB_VC, CB_KS, CB_VS, CB_KW, CB_VW = 32, 34, 36, 38, 40, 42
PROJ_GROUPS = 44
PROJ_COLS = PROJ_GROUPS * HEAD_DIM
PROJ_COLS_PADDED = PROJ_COLS + LANES
ROPE_GROUPS = frozenset(list(range(CB_MQ, CB_MV)) + list(range(CB_NQ, CB_VC))
                        + [CB_KS, CB_KS + 1, CB_KW, CB_KW + 1])
assert PROJ_COLS % MXU_DIM == 0


def _cparams(sem):
    return pltpu.CompilerParams(dimension_semantics=sem, vmem_limit_bytes=VMEM_LIMIT)


def _dot_nt(a, b):
    return lax.dot_general(a, b, (((1,), (1,)), ((), ())), preferred_element_type=F32)


def _rms(x, w):
    ms = jnp.mean(x * x, axis=-1, keepdims=True)
    return (x * lax.rsqrt(ms + EPS)) * w


ADA_TN = 2048
ADA_TK = 1024


def _ada_kernel(ct_ref, w_ref, b_ref, o_ref):
    k = pl.program_id(1)

    @pl.when(k == 0)
    def _():
        o_ref[...] = jnp.broadcast_to(b_ref[...], o_ref.shape)

    ct = ct_ref[...]
    s = ct * jax.nn.sigmoid(ct)
    w = w_ref[...]
    rows = [jnp.sum(w * s[:, b:b + 1], axis=0, keepdims=True) for b in range(ct.shape[1])]
    o_ref[...] += jnp.concatenate(rows, axis=0)


def _ada(c, w_ada, b_ada):
    bsz = c.shape[0]
    n = w_ada.shape[1]
    return pl.pallas_call(
        _ada_kernel,
        out_shape=jax.ShapeDtypeStruct((bsz, n), F32),
        grid=(n // ADA_TN, D_MODEL // ADA_TK),
        in_specs=[
            pl.BlockSpec((ADA_TK, bsz), lambda j, k: (k, 0)),
            pl.BlockSpec((ADA_TK, ADA_TN), lambda j, k: (k, j)),
            pl.BlockSpec((1, ADA_TN), lambda j, k: (0, j)),
        ],
        out_specs=pl.BlockSpec((bsz, ADA_TN), lambda j, k: (0, j)),
        compiler_params=_cparams(("arbitrary", "arbitrary")),
        name="ada",
    )(c.T, w_ada, b_ada.reshape(1, n))


ROPE_TS = 1024


def _rope_kernel(pos_ref, inv_ref, cos_ref, sin_ref):
    half_rows = pos_ref.shape[0] // 2
    lane = lax.broadcasted_iota(jnp.int32, (half_rows, HEAD_DIM), 1)
    low = lane < HALF
    pos = jnp.where(low, pos_ref[:half_rows, :], pos_ref[half_rows:, :]).astype(F32)
    ang = pos * inv_ref[...]
    cs = jnp.cos(ang)
    sn = jnp.sin(ang)
    cs_sw = pltpu.roll(cs, HALF, 1)
    sn_sw = pltpu.roll(sn, HALF, 1)
    cos_ref[:half_rows, :] = jnp.where(low, cs, cs_sw)
    cos_ref[half_rows:, :] = jnp.where(low, cs_sw, cs)
    sin_ref[:half_rows, :] = jnp.where(low, -sn, sn_sw)
    sin_ref[half_rows:, :] = jnp.where(low, -sn_sw, sn)


def _rope_tables(positions):
    rows = positions.size
    inv = ROPE_THETA ** (-jnp.arange(HALF, dtype=F32) / HALF)
    inv = jnp.concatenate([inv, inv]).reshape(1, HEAD_DIM)
    return pl.pallas_call(
        _rope_kernel,
        out_shape=(jax.ShapeDtypeStruct((rows, HEAD_DIM), F32),) * 2,
        grid=(rows // ROPE_TS,),
        in_specs=[
            pl.BlockSpec((ROPE_TS, 1), lambda i: (i, 0)),
            pl.BlockSpec((1, HEAD_DIM), lambda i: (0, 0)),
        ],
        out_specs=(pl.BlockSpec((ROPE_TS, HEAD_DIM), lambda i: (i, 0)),) * 2,
        compiler_params=_cparams(("arbitrary",)),
        name="rope_tables",
    )(positions.reshape(rows, 1), inv)


PROJ_TM = 512


def _inproj_kernel(x_ref, nw_ref, sh_ref, sc_ref, cos_ref, sin_ref, w_ref, o_ref, g_ref):
    h = (_rms(x_ref[...], nw_ref[...]) * (1.0 + sc_ref[0]) + sh_ref[0]).astype(BF16)
    cos = cos_ref[...]
    sin = sin_ref[...]
    for c in range(PROJ_COLS // MXU_DIM):
        acc = jnp.dot(h, w_ref[:, c * MXU_DIM:(c + 1) * MXU_DIM], preferred_element_type=F32)
        for g in range(MXU_DIM // HEAD_DIM):
            grp = c * (MXU_DIM // HEAD_DIM) + g
            y = acc[:, g * HEAD_DIM:(g + 1) * HEAD_DIM]
            if grp in ROPE_GROUPS:
                y = y * cos + pltpu.roll(y, HALF, 1) * sin
            o_ref[:, grp * HEAD_DIM:(grp + 1) * HEAD_DIM] = y.astype(BF16)
    g_ref[...] = jnp.dot(h, w_ref[:, PROJ_COLS:], preferred_element_type=F32)


def _inproj(x2d, nw, mod3, cos, sin, w_in, seq):
    rows = x2d.shape[0]
    tpb = seq // PROJ_TM
    w_pad = jnp.pad(w_in.astype(BF16), ((0, 0), (0, PROJ_COLS_PADDED - w_in.shape[1])))
    return pl.pallas_call(
        _inproj_kernel,
        out_shape=(jax.ShapeDtypeStruct((rows, PROJ_COLS), BF16),
                   jax.ShapeDtypeStruct((rows, LANES), F32)),
        grid=(rows // PROJ_TM,),
        in_specs=[
            pl.BlockSpec((PROJ_TM, D_MODEL), lambda i: (i, 0)),
            pl.BlockSpec((1, D_MODEL), lambda i: (0, 0)),
            pl.BlockSpec((1, 1, D_MODEL), lambda i: ((i // tpb) * 6 + 0, 0, 0)),
            pl.BlockSpec((1, 1, D_MODEL), lambda i: ((i // tpb) * 6 + 1, 0, 0)),
            pl.BlockSpec((PROJ_TM, HEAD_DIM), lambda i: (i, 0)),
            pl.BlockSpec((PROJ_TM, HEAD_DIM), lambda i: (i, 0)),
            pl.BlockSpec((D_MODEL, PROJ_COLS_PADDED), lambda i: (0, 0), pipeline_mode=pl.Buffered(1)),
        ],
        out_specs=(pl.BlockSpec((PROJ_TM, PROJ_COLS), lambda i: (i, 0)),
                   pl.BlockSpec((PROJ_TM, LANES), lambda i: (i, 0))),
        compiler_params=_cparams(("arbitrary",)),
        name="inproj",
    )(x2d, nw, mod3, mod3, cos, sin, w_pad)


EXP2_SCALE = SCALE * 1.4426950408889634


def _lane_tiles(x):
    return [x[:, t * LANES:(t + 1) * LANES] for t in range(x.shape[1] // LANES)]


def _lane_fold(x, op):
    return functools.reduce(op, _lane_tiles(x))


def _ones_augment(v):
    return jnp.concatenate([v, jnp.ones(v.shape, v.dtype)], axis=1)


def _normalize(pv):
    return pv[:, :HEAD_DIM] * (1.0 / pv[:, HEAD_DIM:HEAD_DIM + 1])


def _softmax_pv(s, v_aug):
    m = jnp.max(s, axis=-1, keepdims=True)
    p = jnp.exp2((s - m) * EXP2_SCALE)
    return _normalize(jnp.dot(p.astype(BF16), v_aug, preferred_element_type=F32))


def _two_pass_attention(qk_fn, v_fn, last_mask_fn, n_full, s_scr, m_scr, acc_scr):
    m_scr[...] = jnp.full(m_scr.shape, NEG, F32)

    def pass1_tile(j):
        s = qk_fn(j)
        s_scr[j] = s
        return _lane_fold(s, jnp.maximum)

    def pass1(pair, carry):
        both = jnp.maximum(pass1_tile(2 * pair), pass1_tile(2 * pair + 1))
        m_scr[...] = jnp.maximum(m_scr[...], both)
        return carry

    lax.fori_loop(0, n_full // 2, pass1, 0)

    @pl.when(n_full % 2 == 1)
    def _():
        m_scr[...] = jnp.maximum(m_scr[...], pass1_tile(n_full - 1))

    s = last_mask_fn(qk_fn(n_full))
    s_scr[n_full] = s
    m = jnp.max(jnp.maximum(m_scr[...], _lane_fold(s, jnp.maximum)), axis=-1, keepdims=True)
    m_scr[...] = jnp.broadcast_to(m, m_scr.shape)
    acc_scr[...] = jnp.zeros(acc_scr.shape, F32)

    def pass2_tile(j):
        mb = m_scr[...]
        ps = [jnp.exp2((st - mb) * EXP2_SCALE) for st in _lane_tiles(s_scr[j])]
        p = jnp.concatenate(ps, axis=1).astype(BF16)
        return jnp.dot(p, v_fn(j), preferred_element_type=F32)

    def pass2(pair, carry):
        acc_scr[...] += pass2_tile(2 * pair) + pass2_tile(2 * pair + 1)
        return carry

    n_tiles = n_full + 1
    lax.fori_loop(0, n_tiles // 2, pass2, 0)

    @pl.when(n_tiles % 2 == 1)
    def _():
        acc_scr[...] += pass2_tile(n_tiles - 1)

    return _normalize(acc_scr[...])


def _rank_desc(score, axis, count):
    idx = lax.broadcasted_iota(jnp.int32, score.shape, axis)
    rank = jnp.zeros(score.shape, F32)
    for jp in range(count):
        other = score[:, jp:jp + 1] if axis == 1 else score[jp:jp + 1, :]
        beats = (other > score) | ((other == score) & (idx > jp))
        rank = rank + jnp.where(beats, 1.0, 0.0)
    return rank


MOBA_T = 1024
MOBA_SETUP_CH = 512


def _moba_kernel(q_ref, k_ref, v_ref, o_ref, kmean_scr, qaug_scr, kaug_scr, vaug_scr, s_scr, m_scr, acc_scr,
                 *, nblk):
    qi = pl.program_id(2)
    t = MOBA_T
    blk_shift = MOBA_BLOCK.bit_length() - 1

    @pl.when(qi == 0)
    def _():
        ch = MOBA_SETUP_CH
        kcol = lax.broadcasted_iota(jnp.int32, (ch, LANES), 1)
        krow = lax.broadcasted_iota(jnp.int32, (ch, LANES), 0)
        for n in range(nblk):
            kb = k_ref[n * MOBA_BLOCK:(n + 1) * MOBA_BLOCK, :].astype(F32)
            kmean_scr[n:n + 1, :] = jnp.mean(kb, axis=0, keepdims=True)
        kmean = kmean_scr[...].astype(BF16)
        brow = lax.broadcasted_iota(jnp.int32, (nblk, ch), 0)
        qlane = lax.broadcasted_iota(jnp.int32, (nblk, ch), 1)
        for c in range(k_ref.shape[0] // ch):
            rows = slice(c * ch, (c + 1) * ch)
            kaug_scr[rows, :HEAD_DIM] = k_ref[rows, :]
            blk = jnp.right_shift(c * ch + krow, blk_shift)
            kaug_scr[rows, HEAD_DIM:] = jnp.where(kcol == blk, 1.0, 0.0).astype(BF16)
            vaug_scr[rows, :] = _ones_augment(v_ref[rows, :])
            q = q_ref[rows, :]
            qblk = jnp.right_shift(c * ch + qlane, blk_shift)
            past = brow < qblk
            gate = jnp.where(past, _dot_nt(kmean, q), NEG)
            rank = _rank_desc(gate, 0, nblk)
            allow = (past & (rank < float(MOBA_TOPK))) | (brow == qblk)
            bias_t = jnp.where(allow, 0.0, NEG)
            bias_t = jnp.concatenate([bias_t, jnp.zeros((LANES - nblk, ch), F32)], axis=0)
            qaug_scr[rows, :HEAD_DIM] = q
            qaug_scr[rows, HEAD_DIM:] = bias_t.T.astype(BF16)

    q_aug = qaug_scr[pl.ds(pl.multiple_of(qi * t, t), t), :]

    def qk(j):
        return _dot_nt(q_aug, kaug_scr[pl.ds(pl.multiple_of(j * t, t), t), :])

    def vt(j):
        return vaug_scr[pl.ds(pl.multiple_of(j * t, t), t), :]

    def causal(s):
        r = lax.broadcasted_iota(jnp.int32, s.shape, 0)
        c = lax.broadcasted_iota(jnp.int32, s.shape, 1)
        return jnp.where(c <= r, s, NEG)

    o = _two_pass_attention(qk, vt, causal, qi, s_scr, m_scr, acc_scr)
    o_ref[...] = o.astype(BF16)


def _moba(proj, bsz, seq):
    nq = seq // MOBA_T
    nblk = seq // MOBA_BLOCK
    assert nblk % 8 == 0 and nblk <= LANES
    return pl.pallas_call(
        functools.partial(_moba_kernel, nblk=nblk),
        out_shape=jax.ShapeDtypeStruct((bsz * seq, MOBA_WIDTH), BF16),
        grid=(bsz, MOBA_HEADS, nq),
        in_specs=[
            pl.BlockSpec((seq, HEAD_DIM), lambda b, h, i: (b, CB_MQ + h)),
            pl.BlockSpec((seq, HEAD_DIM), lambda b, h, i: (b, CB_MK + h)),
            pl.BlockSpec((seq, HEAD_DIM), lambda b, h, i: (b, CB_MV + h)),
        ],
        out_specs=pl.BlockSpec((MOBA_T, HEAD_DIM), lambda b, h, i: (b * nq + i, h)),
        scratch_shapes=[
            pltpu.VMEM((nblk, HEAD_DIM), F32),
            pltpu.VMEM((seq, 2 * HEAD_DIM), BF16),
            pltpu.VMEM((seq, 2 * HEAD_DIM), BF16),
            pltpu.VMEM((seq, 2 * HEAD_DIM), BF16),
            pltpu.VMEM((nq, MOBA_T, MOBA_T), F32),
            pltpu.VMEM((MOBA_T, LANES), F32),
            pltpu.VMEM((MOBA_T, 2 * HEAD_DIM), F32),
        ],
        compiler_params=_cparams(("arbitrary", "arbitrary", "arbitrary")),
        name="moba",
    )(proj, proj, proj)


CMP_HALF_FEAT = CMP_STRIDE * HEAD_DIM


def _compress_kernel(x_ref, pos_ref, w1f_ref, w1_ref, w2_ref, o_ref):
    x = x_ref[0, 0]
    a = jnp.dot(x, w1_ref[:CMP_HALF_FEAT, :], preferred_element_type=F32)
    b = jnp.dot(x, w1_ref[CMP_HALF_FEAT:, :], preferred_element_type=F32)
    nch = x.shape[0]
    b_next = pltpu.roll(b, nch - 1, 0)
    pos_bias = jnp.sum(w1f_ref[...] * pos_ref[...], axis=0, keepdims=True)
    hid = jax.nn.gelu(a + b_next + pos_bias, approximate=True)
    o_ref[0, 0] = jnp.dot(hid.astype(BF16), w2_ref[...], preferred_element_type=F32).astype(BF16)


def _compress(xc, pos_col, w1, w2):
    bsz, hk, nch, feat = xc.shape
    return pl.pallas_call(
        _compress_kernel,
        out_shape=jax.ShapeDtypeStruct((bsz, hk, nch, HEAD_DIM), BF16),
        grid=(bsz, hk),
        in_specs=[
            pl.BlockSpec((1, 1, nch, feat), lambda b, k: (b, k, 0, 0)),
            pl.BlockSpec((CMP_LEN * HEAD_DIM, 1), lambda b, k: (0, 0)),
            pl.BlockSpec((CMP_LEN * HEAD_DIM, CMP_HIDDEN), lambda b, k: (0, 0)),
            pl.BlockSpec((CMP_LEN * HEAD_DIM, CMP_HIDDEN), lambda b, k: (0, 0)),
            pl.BlockSpec((CMP_HIDDEN, HEAD_DIM), lambda b, k: (0, 0)),
        ],
        out_specs=pl.BlockSpec((1, 1, nch, HEAD_DIM), lambda b, k: (b, k, 0, 0)),
        compiler_params=_cparams(("arbitrary", "arbitrary")),
        name="compress",
    )(xc, pos_col, w1, w1.astype(BF16), w2.astype(BF16))


NSA_TQ = 256
NSA_TK = 512
NSA_ROWS = NSA_GROUP * NSA_TQ


def _split3(x):
    hi = x.astype(BF16)
    r1 = x - hi.astype(F32)
    mid = r1.astype(BF16)
    lo = (r1 - mid.astype(F32)).astype(BF16)
    return hi, mid, lo


def _nsa_kernel(q_ref, kc_ref, vc_ref, ks_ref, vs_ref, kw_ref, vw_ref, gl_ref, mt_ref, wb_ref, o_ref,
                kaug_scr, vsaug_scr, vwaug_scr, s_scr, m_scr, acc_scr):
    qi = pl.program_id(2)
    tq = NSA_TQ
    tk = NSA_TK
    seq = ks_ref.shape[0]
    slc_shift = SLC_BLOCK.bit_length() - 1

    @pl.when(qi == 0)
    def _():
        kcol = lax.broadcasted_iota(jnp.int32, (tk, LANES), 1)
        krow = lax.broadcasted_iota(jnp.int32, (tk, LANES), 0)
        for c in range(seq // tk):
            rows = slice(c * tk, (c + 1) * tk)
            kaug_scr[rows, :HEAD_DIM] = ks_ref[rows, :]
            blk = jnp.right_shift(c * tk + krow, slc_shift)
            kaug_scr[rows, HEAD_DIM:] = jnp.where(kcol == blk, 1.0, 0.0).astype(BF16)
            vsaug_scr[rows, :] = _ones_augment(vs_ref[rows, :])
            vwaug_scr[rows, :] = _ones_augment(vw_ref[rows, :])

    q = q_ref[...]
    qg = jnp.concatenate([q[:, g * HEAD_DIM:(g + 1) * HEAD_DIM] for g in range(NSA_GROUP)], axis=0)
    ncmp = kc_ref.shape[2]

    def qpos_of(shape):
        return qi * tq + jnp.bitwise_and(lax.broadcasted_iota(jnp.int32, shape, 0), tq - 1)

    span = WINDOW + tq
    start = pl.multiple_of(jnp.maximum(qi * tq - WINDOW, 0), tq)
    s = _dot_nt(qg, kw_ref[pl.ds(start, span), :])
    s = (s.reshape(NSA_GROUP, tq, span) + wb_ref[...]).reshape(NSA_ROWS, span)
    o_win = _softmax_pv(s, vwaug_scr[pl.ds(start, span), :])

    colc = lax.broadcasted_iota(jnp.int32, (NSA_ROWS, ncmp), 1)
    cmask = colc * CMP_STRIDE + (CMP_LEN - 1) <= qpos_of((NSA_ROWS, ncmp))
    s = jnp.where(cmask, _dot_nt(qg, kc_ref[0, 0]), NEG)
    e = jnp.exp2((s - jnp.max(s, axis=-1, keepdims=True)) * EXP2_SCALE)
    has_blk = qpos_of((NSA_ROWS, 1)) >= CMP_LEN - 1
    p = e * jnp.where(has_blk, 1.0 / jnp.sum(e, axis=-1, keepdims=True), 0.0)
    o_cmp = jnp.dot(p.astype(BF16), vc_ref[0, 0], preferred_element_type=F32)

    imp = p[0:tq]
    for g in range(1, NSA_GROUP):
        imp = imp + p[g * tq:(g + 1) * tq]
    n_slc = seq // SLC_BLOCK
    mt = mt_ref[...]
    p_slc = sum(_dot_nt(mt, part) for part in _split3(imp))
    jrow = lax.broadcasted_iota(jnp.int32, p_slc.shape, 0)
    jt = jnp.right_shift(qi * tq + lax.broadcasted_iota(jnp.int32, p_slc.shape, 1), slc_shift)
    valid = jrow <= jt
    forced = (jrow == 0) | (jrow == jt) | (jrow == jt - 1)
    score = jnp.where(valid & forced, BIG, jnp.where(valid, p_slc, NEG))
    rank = _rank_desc(score, 0, n_slc)
    sel = valid & (rank < float(min(SLC_TOPK, n_slc)))
    bias_t = jnp.where(sel, 0.0, NEG)
    bias_t = jnp.concatenate([bias_t, jnp.zeros((LANES - n_slc, tq), F32)], axis=0)
    bias = bias_t.T.astype(BF16)
    q_aug = jnp.concatenate([qg, jnp.concatenate([bias] * NSA_GROUP, axis=0)], axis=1)

    def slc_qk(j):
        return _dot_nt(q_aug, kaug_scr[pl.ds(pl.multiple_of(j * tk, tk), tk), :])

    def slc_v(j):
        return vsaug_scr[pl.ds(pl.multiple_of(j * tk, tk), tk), :]

    n_full = (qi * tq) // tk

    def slc_causal(s):
        kpos = n_full * tk + lax.broadcasted_iota(jnp.int32, s.shape, 1)
        return jnp.where(kpos <= qpos_of(s.shape), s, NEG)

    o_slc = _two_pass_attention(slc_qk, slc_v, slc_causal, n_full, s_scr, m_scr, acc_scr)

    sig = jax.nn.sigmoid(gl_ref[...])
    kv_head = pl.program_id(1)

    def gate_col(branch, g):
        cols = [sig[:, branch * NSA_HEADS + k * NSA_GROUP + g:branch * NSA_HEADS + k * NSA_GROUP + g + 1]
                for k in range(NSA_KV_HEADS)]
        col = cols[0]
        for k in range(1, NSA_KV_HEADS):
            col = jnp.where(kv_head == k, cols[k], col)
        return col

    for g in range(NSA_GROUP):
        rows = slice(g * tq, (g + 1) * tq)
        og = (gate_col(0, g) * o_cmp[rows] + gate_col(1, g) * o_slc[rows] + gate_col(2, g) * o_win[rows])
        o_ref[:, g * HEAD_DIM:(g + 1) * HEAD_DIM] = og.astype(BF16)


def _slc_weight_matrix(n_slc, ncmp):
    rs, rc = SLC_BLOCK // CMP_STRIDE, CMP_LEN // CMP_STRIDE
    j = jnp.arange(n_slc)[:, None]
    n = jnp.arange(ncmp)[None, :]
    w = jnp.zeros((n_slc, ncmp), F32)
    for o in range(rs + rc - 1):
        w_o = float(sum(1 for m in range(rs) for nn in range(rc) if m - nn + rc - 1 == o))
        w = w + jnp.where(n == rs * j + o - (rc - 1), w_o, 0.0)
    return w.astype(BF16)


def _window_bias():
    span = WINDOW + NSA_TQ
    cases = WINDOW // NSA_TQ + 1
    c = jnp.arange(cases)[:, None, None]
    qpos = c * NSA_TQ + jnp.arange(NSA_TQ)[None, :, None]
    kpos = jnp.maximum(c * NSA_TQ - WINDOW, 0) + jnp.arange(span)[None, None, :]
    return jnp.where((kpos <= qpos) & (kpos > qpos - WINDOW), 0.0, NEG).astype(F32)


def _nsa(proj, gates, kc, vc, bsz, seq):
    nq = seq // NSA_TQ
    ncmp = kc.shape[2]
    n_slc = seq // SLC_BLOCK
    assert n_slc % 8 == 0 and n_slc <= LANES and seq % NSA_TK == 0
    wbias = _window_bias()
    last_case = wbias.shape[0] - 1
    qw = NSA_GROUP * HEAD_DIM
    kv_spec = lambda cb: pl.BlockSpec((seq, HEAD_DIM), lambda b, k, i: (b, cb + k))
    cmp_spec = pl.BlockSpec((1, 1, ncmp, HEAD_DIM), lambda b, k, i: (b, k, 0, 0))
    return pl.pallas_call(
        _nsa_kernel,
        out_shape=jax.ShapeDtypeStruct((bsz * seq, NSA_WIDTH), BF16),
        grid=(bsz, NSA_KV_HEADS, nq),
        in_specs=[
            pl.BlockSpec((NSA_TQ, qw), lambda b, k, i: (b * nq + i, CB_NQ * HEAD_DIM // qw + k)),
            cmp_spec, cmp_spec,
            kv_spec(CB_KS), kv_spec(CB_VS), kv_spec(CB_KW), kv_spec(CB_VW),
            pl.BlockSpec((NSA_TQ, LANES), lambda b, k, i: (b * nq + i, 0)),
            pl.BlockSpec((n_slc, ncmp), lambda b, k, i: (0, 0)),
            pl.BlockSpec((1,) + wbias.shape[1:], lambda b, k, i: (jnp.minimum(i, last_case), 0, 0)),
        ],
        out_specs=pl.BlockSpec((NSA_TQ, qw), lambda b, k, i: (b * nq + i, k)),
        scratch_shapes=[
            pltpu.VMEM((seq, 2 * HEAD_DIM), BF16),
            pltpu.VMEM((seq, 2 * HEAD_DIM), BF16),
            pltpu.VMEM((seq, 2 * HEAD_DIM), BF16),
            pltpu.VMEM((seq // NSA_TK, NSA_ROWS, NSA_TK), F32),
            pltpu.VMEM((NSA_ROWS, LANES), F32),
            pltpu.VMEM((NSA_ROWS, 2 * HEAD_DIM), F32),
        ],
        compiler_params=_cparams(("arbitrary", "arbitrary", "arbitrary")),
        name="nsa",
    )(proj, kc, vc, proj, proj, proj, proj, gates, _slc_weight_matrix(n_slc, ncmp), wbias)


OUT_TM = 512


def _outproj_kernel(om_ref, on_ref, wm_ref, wn_ref, x_ref, ga_ref, npost_ref, npre_ref, sh_ref, sc_ref,
                    x1_ref, h2_ref):
    y = (jnp.dot(om_ref[...], wm_ref[...], preferred_element_type=F32)
         + jnp.dot(on_ref[...], wn_ref[...], preferred_element_type=F32))
    x1 = x_ref[...] + ga_ref[0] * _rms(y, npost_ref[...])
    x1_ref[...] = x1
    h2_ref[...] = (_rms(x1, npre_ref[...]) * (1.0 + sc_ref[0]) + sh_ref[0]).astype(BF16)


def _outproj(o_moba, o_nsa, w_out, x2d, mod3, npost, npre, seq):
    rows = x2d.shape[0]
    tpb = seq // OUT_TM
    mod_spec = lambda idx: pl.BlockSpec((1, 1, D_MODEL), lambda i: ((i // tpb) * 6 + idx, 0, 0))
    vec_spec = pl.BlockSpec((1, D_MODEL), lambda i: (0, 0))
    w_bf = w_out.astype(BF16)
    return pl.pallas_call(
        _outproj_kernel,
        out_shape=(jax.ShapeDtypeStruct((rows, D_MODEL), F32),
                   jax.ShapeDtypeStruct((rows, D_MODEL), BF16)),
        grid=(rows // OUT_TM,),
        in_specs=[
            pl.BlockSpec((OUT_TM, MOBA_WIDTH), lambda i: (i, 0)),
            pl.BlockSpec((OUT_TM, NSA_WIDTH), lambda i: (i, 0)),
            pl.BlockSpec((MOBA_WIDTH, D_MODEL), lambda i: (0, 0)),
            pl.BlockSpec((NSA_WIDTH, D_MODEL), lambda i: (0, 0)),
            pl.BlockSpec((OUT_TM, D_MODEL), lambda i: (i, 0)),
            mod_spec(2), vec_spec, vec_spec, mod_spec(3), mod_spec(4),
        ],
        out_specs=(pl.BlockSpec((OUT_TM, D_MODEL), lambda i: (i, 0)),
                   pl.BlockSpec((OUT_TM, D_MODEL), lambda i: (i, 0))),
        compiler_params=_cparams(("arbitrary",)),
        name="outproj",
    )(o_moba, o_nsa, w_bf[:MOBA_WIDTH], w_bf[MOBA_WIDTH:], x2d, mod3, npost, npre, mod3, mod3)


FFN_TM = 512
FFN_TF = 512
FFN_HALO = 16
CONV_WIDTH = 3


def _ffn_kernel(h_ref, halo_ref, wg_ref, wv_ref, cwg_ref, cwv_ref, cbg_ref, cbv_ref, wd_ref,
                x1_ref, gf_ref, npost_ref, o_ref, acc_scr, u_scr, haug_scr, *, tiles_per_seq):
    i = pl.program_id(0)
    j = pl.program_id(1)

    @pl.when(j == 0)
    def _():
        acc_scr[...] = jnp.zeros(acc_scr.shape, F32)
        halo = halo_ref[...]
        haug_scr[:FFN_HALO, :] = jnp.where(i % tiles_per_seq == 0, jnp.zeros_like(halo), halo)
        haug_scr[FFN_HALO:, :] = h_ref[...]

    h_aug = haug_scr[...]

    def conv_branch(w_ref, cw_ref, cb_ref):
        u_scr[...] = jnp.dot(h_aug, w_ref[...], preferred_element_type=F32)
        cw = cw_ref[...]
        out = cb_ref[...]
        for tap in range(CONV_WIDTH):
            ofs = FFN_HALO - (CONV_WIDTH - 1) + tap
            out = out + cw[tap:tap + 1, :] * u_scr[pl.ds(ofs, FFN_TM), :]
        return out

    gate = conv_branch(wg_ref, cwg_ref, cbg_ref)
    val = conv_branch(wv_ref, cwv_ref, cbv_ref)
    act = (jax.nn.gelu(gate, approximate=True) * val).astype(BF16)
    acc_scr[...] += jnp.dot(act, wd_ref[...], preferred_element_type=F32)

    @pl.when(j == pl.num_programs(1) - 1)
    def _():
        o_ref[...] = x1_ref[...] + gf_ref[0] * _rms(acc_scr[...], npost_ref[...])


def _ffn(h2, x1, w_up, conv_w, conv_b, w_down, mod3, npost, seq):
    rows = h2.shape[0]
    tps = seq // FFN_TM
    nj = D_FF // FFN_TF
    w_up_bf = w_up.astype(BF16)
    cb = conv_b.reshape(1, 2 * D_FF)
    return pl.pallas_call(
        functools.partial(_ffn_kernel, tiles_per_seq=tps),
        out_shape=jax.ShapeDtypeStruct((rows, D_MODEL), F32),
        grid=(rows // FFN_TM, nj),
        in_specs=[
            pl.BlockSpec((FFN_TM, D_MODEL), lambda i, j: (i, 0)),
            pl.BlockSpec((FFN_HALO, D_MODEL),
                         lambda i, j: (jnp.maximum(i * (FFN_TM // FFN_HALO) - 1, 0), 0)),
            pl.BlockSpec((D_MODEL, FFN_TF), lambda i, j: (0, j)),
            pl.BlockSpec((D_MODEL, FFN_TF), lambda i, j: (0, nj + j)),
            pl.BlockSpec((CONV_WIDTH, FFN_TF), lambda i, j: (0, j)),
            pl.BlockSpec((CONV_WIDTH, FFN_TF), lambda i, j: (0, nj + j)),
            pl.BlockSpec((1, FFN_TF), lambda i, j: (0, j)),
            pl.BlockSpec((1, FFN_TF), lambda i, j: (0, nj + j)),
            pl.BlockSpec((FFN_TF, D_MODEL), lambda i, j: (j, 0)),
            pl.BlockSpec((FFN_TM, D_MODEL), lambda i, j: (i, 0)),
            pl.BlockSpec((1, 1, D_MODEL), lambda i, j: ((i // tps) * 6 + 5, 0, 0)),
            pl.BlockSpec((1, D_MODEL), lambda i, j: (0, 0)),
        ],
        out_specs=pl.BlockSpec((FFN_TM, D_MODEL), lambda i, j: (i, 0)),
        scratch_shapes=[
            pltpu.VMEM((FFN_TM, D_MODEL), F32),
            pltpu.VMEM((FFN_HALO + FFN_TM, FFN_TF), F32),
            pltpu.VMEM((FFN_HALO + FFN_TM, D_MODEL), BF16),
        ],
        compiler_params=_cparams(("arbitrary", "arbitrary")),
        name="ffn",
    )(h2, h2, w_up_bf, w_up_bf, conv_w, conv_w, cb, cb, w_down.astype(BF16), x1, mod3, npost)


def _chunks(proj, cb, bsz, seq):
    a = proj[:, cb * HEAD_DIM:(cb + NSA_KV_HEADS) * HEAD_DIM]
    a = a.reshape(bsz, seq // CMP_STRIDE, CMP_STRIDE, NSA_KV_HEADS, HEAD_DIM).transpose(0, 3, 1, 2, 4)
    return a.reshape(bsz, NSA_KV_HEADS, seq // CMP_STRIDE, CMP_HALF_FEAT)


def kernel(x, c, positions, w_ada, b_ada, norm_pre_mix, norm_post_mix, norm_pre_ffn, norm_post_ffn,
           w_in, w_out, cmp_pos_k, cmp_w1_k, cmp_w2_k, cmp_pos_v, cmp_w1_v, cmp_w2_v,
           w_up, conv_w, conv_b, w_down):
    bsz, seq, d = x.shape
    depth = w_in.shape[0]
    assert d == D_MODEL and seq % WINDOW == 0 and seq % PROJ_TM == 0
    cos, sin = _rope_tables(positions)
    x2d = x.reshape(bsz * seq, d)
    for l in range(depth):
        mod3 = _ada(c, w_ada[l], b_ada[l]).reshape(bsz * 6, 1, d)
        assert w_in.shape[2] == PROJ_COLS + 3 * NSA_HEADS
        proj, gates = _inproj(x2d, norm_pre_mix[l].reshape(1, d), mod3, cos, sin, w_in[l], seq)
        o_moba = _moba(proj, bsz, seq)
        kc = _compress(_chunks(proj, CB_KC, bsz, seq), cmp_pos_k[l].reshape(-1, 1), cmp_w1_k[l], cmp_w2_k[l])
        vc = _compress(_chunks(proj, CB_VC, bsz, seq), cmp_pos_v[l].reshape(-1, 1), cmp_w1_v[l], cmp_w2_v[l])
        o_nsa = _nsa(proj, gates, kc, vc, bsz, seq)
        x1, h2 = _outproj(o_moba, o_nsa, w_out[l], x2d, mod3, norm_post_mix[l].reshape(1, d),
                          norm_pre_ffn[l].reshape(1, d), seq)
        x2d = _ffn(h2, x1, w_up[l], conv_w[l], conv_b[l], w_down[l], mod3,
                   norm_post_ffn[l].reshape(1, d), seq)
    return x2d.reshape(bsz, seq, d)
```

```python
import functools

import jax
import jax.numpy as jnp
from jax import lax
from jax.experimental import pallas as pl
from jax.experimental.pallas import tpu as pltpu

F32 = jnp.float32
BF16 = jnp.bfloat16

D_MODEL = 2048
HEAD_DIM = 128
HALF = HEAD_DIM // 2
MOBA_HEADS = 8
NSA_HEADS = 8
NSA_KV_HEADS = 2
NSA_GROUP = NSA_HEADS // NSA_KV_HEADS
MOBA_BLOCK = 256
MOBA_TOPK = 3
CMP_LEN = 32
CMP_STRIDE = 16
CMP_HIDDEN = 2 * HEAD_DIM
SLC_BLOCK = 64
SLC_TOPK = 16
WINDOW = 512
D_FF = 4 * D_MODEL
ROPE_THETA = 10000.0
EPS = 1e-6
NEG = -1e30
BIG = 1e9
SCALE = HEAD_DIM ** -0.5

MOBA_WIDTH = MOBA_HEADS * HEAD_DIM
NSA_WIDTH = NSA_HEADS * HEAD_DIM
KV_WIDTH = NSA_KV_HEADS * HEAD_DIM

LANES = 128
MXU_DIM = 256
VMEM_LIMIT = 56 * 1024 * 1024

CB_MQ, CB_MK, CB_MV, CB_NQ = 0, 8, 16, 24
CB_KC, CB_VC, CB_KS, CB_VS, CB_KW, CB_VW = 32, 34, 36, 38, 40, 42
PROJ_GROUPS = 44
PROJ_COLS = PROJ_GROUPS * HEAD_DIM
PROJ_COLS_PADDED = PROJ_COLS + LANES
ROPE_GROUPS = frozenset(list(range(CB_MQ, CB_MV)) + list(range(CB_NQ, CB_VC))
                        + [CB_KS, CB_KS + 1, CB_KW, CB_KW + 1])
assert PROJ_COLS % MXU_DIM == 0


def _cparams(sem):
    return pltpu.CompilerParams(dimension_semantics=sem, vmem_limit_bytes=VMEM_LIMIT)


def _dot_nt(a, b):
    return lax.dot_general(a, b, (((1,), (1,)), ((), ())), preferred_element_type=F32)


def _rms(x, w):
    ms = jnp.mean(x * x, axis=-1, keepdims=True)
    return (x * lax.rsqrt(ms + EPS)) * w


ADA_TN = 2048
ADA_TK = 1024


def _ada_kernel(ct_ref, w_ref, b_ref, o_ref):
    k = pl.program_id(1)

    @pl.when(k == 0)
    def _():
        o_ref[...] = jnp.broadcast_to(b_ref[...], o_ref.shape)

    ct = ct_ref[...]
    s = ct * jax.nn.sigmoid(ct)
    w = w_ref[...]
    rows = [jnp.sum(w * s[:, b:b + 1], axis=0, keepdims=True) for b in range(ct.shape[1])]
    o_ref[...] += jnp.concatenate(rows, axis=0)


def _ada(c, w_ada, b_ada):
    bsz = c.shape[0]
    n = w_ada.shape[1]
    return pl.pallas_call(
        _ada_kernel,
        out_shape=jax.ShapeDtypeStruct((bsz, n), F32),
        grid=(n // ADA_TN, D_MODEL // ADA_TK),
        in_specs=[
            pl.BlockSpec((ADA_TK, bsz), lambda j, k: (k, 0)),
            pl.BlockSpec((ADA_TK, ADA_TN), lambda j, k: (k, j)),
            pl.BlockSpec((1, ADA_TN), lambda j, k: (0, j)),
        ],
        out_specs=pl.BlockSpec((bsz, ADA_TN), lambda j, k: (0, j)),
        compiler_params=_cparams(("arbitrary", "arbitrary")),
        name="ada",
    )(c.T, w_ada, b_ada.reshape(1, n))


ROPE_TS = 1024


def _rope_kernel(pos_ref, inv_ref, cos_ref, sin_ref):
    half_rows = pos_ref.shape[0] // 2
    lane = lax.broadcasted_iota(jnp.int32, (half_rows, HEAD_DIM), 1)
    low = lane < HALF
    pos = jnp.where(low, pos_ref[:half_rows, :], pos_ref[half_rows:, :]).astype(F32)
    ang = pos * inv_ref[...]
    cs = jnp.cos(ang)
    sn = jnp.sin(ang)
    cs_sw = pltpu.roll(cs, HALF, 1)
    sn_sw = pltpu.roll(sn, HALF, 1)
    cos_ref[:half_rows, :] = jnp.where(low, cs, cs_sw)
    cos_ref[half_rows:, :] = jnp.where(low, cs_sw, cs)
    sin_ref[:half_rows, :] = jnp.where(low, -sn, sn_sw)
    sin_ref[half_rows:, :] = jnp.where(low, -sn_sw, sn)


def _rope_tables(positions):
    rows = positions.size
    inv = ROPE_THETA ** (-jnp.arange(HALF, dtype=F32) / HALF)
    inv = jnp.concatenate([inv, inv]).reshape(1, HEAD_DIM)
    return pl.pallas_call(
        _rope_kernel,
        out_shape=(jax.ShapeDtypeStruct((rows, HEAD_DIM), F32),) * 2,
        grid=(rows // ROPE_TS,),
        in_specs=[
            pl.BlockSpec((ROPE_TS, 1), lambda i: (i, 0)),
            pl.BlockSpec((1, HEAD_DIM), lambda i: (0, 0)),
        ],
        out_specs=(pl.BlockSpec((ROPE_TS, HEAD_DIM), lambda i: (i, 0)),) * 2,
        compiler_params=_cparams(("arbitrary",)),
        name="rope_tables",
    )(positions.reshape(rows, 1), inv)


PROJ_TM = 512


def _inproj_kernel(x_ref, nw_ref, sh_ref, sc_ref, cos_ref, sin_ref, w_ref, wgate_ref, o_ref, g_ref):
    h = (_rms(x_ref[...], nw_ref[...]) * (1.0 + sc_ref[0]) + sh_ref[0]).astype(BF16)
    cos = cos_ref[...]
    sin = sin_ref[...]
    for c in range(PROJ_COLS // MXU_DIM):
        acc = jnp.dot(h, w_ref[:, c * MXU_DIM:(c + 1) * MXU_DIM], preferred_element_type=F32)
        for g in range(MXU_DIM // HEAD_DIM):
            grp = c * (MXU_DIM // HEAD_DIM) + g
            y = acc[:, g * HEAD_DIM:(g + 1) * HEAD_DIM]
            if grp in ROPE_GROUPS:
                y = y * cos + pltpu.roll(y, HALF, 1) * sin
            o_ref[:, grp * HEAD_DIM:(grp + 1) * HEAD_DIM] = y.astype(BF16)
    g_ref[...] = jnp.dot(h, wgate_ref[...], preferred_element_type=F32)


def _inproj(x2d, nw, mod3, cos, sin, w_in, seq):
    rows = x2d.shape[0]
    tpb = seq // PROJ_TM
    w_main = w_in[:, :PROJ_COLS].astype(BF16)
    w_gate = jnp.pad(w_in[:, PROJ_COLS:], ((0, 0), (0, PROJ_COLS_PADDED - w_in.shape[1]))).astype(BF16)
    return pl.pallas_call(
        _inproj_kernel,
        out_shape=(jax.ShapeDtypeStruct((rows, PROJ_COLS), BF16),
                   jax.ShapeDtypeStruct((rows, LANES), F32)),
        grid=(rows // PROJ_TM,),
        in_specs=[
            pl.BlockSpec((PROJ_TM, D_MODEL), lambda i: (i, 0)),
            pl.BlockSpec((1, D_MODEL), lambda i: (0, 0)),
            pl.BlockSpec((1, 1, D_MODEL), lambda i: ((i // tpb) * 6 + 0, 0, 0)),
            pl.BlockSpec((1, 1, D_MODEL), lambda i: ((i // tpb) * 6 + 1, 0, 0)),
            pl.BlockSpec((PROJ_TM, HEAD_DIM), lambda i: (i, 0)),
            pl.BlockSpec((PROJ_TM, HEAD_DIM), lambda i: (i, 0)),
            pl.BlockSpec((D_MODEL, PROJ_COLS), lambda i: (0, 0), pipeline_mode=pl.Buffered(1)),
            pl.BlockSpec((D_MODEL, LANES), lambda i: (0, 0)),
        ],
        out_specs=(pl.BlockSpec((PROJ_TM, PROJ_COLS), lambda i: (i, 0)),
                   pl.BlockSpec((PROJ_TM, LANES), lambda i: (i, 0))),
        compiler_params=_cparams(("arbitrary",)),
        name="inproj",
    )(x2d, nw, mod3, mod3, cos, sin, w_main, w_gate)


EXP2_SCALE = SCALE * 1.4426950408889634


def _lane_tiles(x):
    return [x[:, t * LANES:(t + 1) * LANES] for t in range(x.shape[1] // LANES)]


def _lane_fold(x, op):
    return functools.reduce(op, _lane_tiles(x))


def _ones_augment(v):
    return jnp.concatenate([v, jnp.ones(v.shape, v.dtype)], axis=1)


def _normalize(pv):
    return pv[:, :HEAD_DIM] * (1.0 / pv[:, HEAD_DIM:HEAD_DIM + 1])


def _softmax_pv(s, v_aug):
    m = jnp.max(s, axis=-1, keepdims=True)
    p = jnp.exp2((s - m) * EXP2_SCALE)
    return _normalize(jnp.dot(p.astype(BF16), v_aug, preferred_element_type=F32))


def _two_pass_attention(qk_fn, v_fn, last_mask_fn, n_full, s_scr, m_scr, acc_scr):
    m_scr[...] = jnp.full(m_scr.shape, NEG, F32)

    def pass1_tile(j):
        s = qk_fn(j, 1)
        s_scr[j] = s
        return _lane_fold(s, jnp.maximum)

    def pass1(pair, carry):
        s = qk_fn(pair, 2)
        tk = s.shape[1] // 2
        s_scr[2 * pair] = s[:, :tk]
        s_scr[2 * pair + 1] = s[:, tk:]
        m_scr[...] = jnp.maximum(m_scr[...], _lane_fold(s, jnp.maximum))
        return carry

    lax.fori_loop(0, n_full // 2, pass1, 0)

    @pl.when(n_full % 2 == 1)
    def _():
        m_scr[...] = jnp.maximum(m_scr[...], pass1_tile(n_full - 1))

    s = last_mask_fn(qk_fn(n_full, 1))
    s_scr[n_full] = s
    m = jnp.max(jnp.maximum(m_scr[...], _lane_fold(s, jnp.maximum)), axis=-1, keepdims=True)
    m_scr[...] = jnp.broadcast_to(m, m_scr.shape)
    acc_scr[...] = jnp.zeros(acc_scr.shape, F32)

    def pass2_tile(j):
        mb = m_scr[...]
        ps = [jnp.exp2((st - mb) * EXP2_SCALE) for st in _lane_tiles(s_scr[j])]
        p = jnp.concatenate(ps, axis=1).astype(BF16)
        return jnp.dot(p, v_fn(j, 1), preferred_element_type=F32)

    def pass2(pair, carry):
        mb = m_scr[...]
        ps = [jnp.exp2((st - mb) * EXP2_SCALE)
              for t in (2 * pair, 2 * pair + 1) for st in _lane_tiles(s_scr[t])]
        p = jnp.concatenate(ps, axis=1).astype(BF16)
        acc_scr[...] += jnp.dot(p, v_fn(pair, 2), preferred_element_type=F32)
        return carry

    n_tiles = n_full + 1
    lax.fori_loop(0, n_tiles // 2, pass2, 0)

    @pl.when(n_tiles % 2 == 1)
    def _():
        acc_scr[...] += pass2_tile(n_tiles - 1)

    return _normalize(acc_scr[...])


def _rank_desc(score, axis, count):
    idx = lax.broadcasted_iota(jnp.int32, score.shape, axis)
    rank = jnp.zeros(score.shape, F32)
    for jp in range(count):
        other = score[:, jp:jp + 1] if axis == 1 else score[jp:jp + 1, :]
        beats = (other > score) | ((other == score) & (idx > jp))
        rank = rank + jnp.where(beats, 1.0, 0.0)
    return rank


MOBA_T = 1024
MOBA_SETUP_CH = 512


def _moba_kernel(q_ref, k_ref, v_ref, o_ref, kmean_scr, qaug_scr, kaug_scr, vaug_scr, s_scr, m_scr, acc_scr,
                 *, nblk):
    qi = pl.program_id(2)
    t = MOBA_T
    blk_shift = MOBA_BLOCK.bit_length() - 1

    @pl.when(qi == 0)
    def _():
        ch = MOBA_SETUP_CH
        kcol = lax.broadcasted_iota(jnp.int32, (ch, LANES), 1)
        krow = lax.broadcasted_iota(jnp.int32, (ch, LANES), 0)
        for n in range(nblk):
            kb = k_ref[n * MOBA_BLOCK:(n + 1) * MOBA_BLOCK, :].astype(F32)
            kmean_scr[n:n + 1, :] = jnp.mean(kb, axis=0, keepdims=True)
        kmean = kmean_scr[...].astype(BF16)
        brow = lax.broadcasted_iota(jnp.int32, (nblk, ch), 0)
        qlane = lax.broadcasted_iota(jnp.int32, (nblk, ch), 1)
        for c in range(k_ref.shape[0] // ch):
            rows = slice(c * ch, (c + 1) * ch)
            kaug_scr[rows, :HEAD_DIM] = k_ref[rows, :]
            blk = jnp.right_shift(c * ch + krow, blk_shift)
            kaug_scr[rows, HEAD_DIM:] = jnp.where(kcol == blk, 1.0, 0.0).astype(BF16)
            vaug_scr[rows, :] = _ones_augment(v_ref[rows, :])
            q = q_ref[rows, :]
            qblk = jnp.right_shift(c * ch + qlane, blk_shift)
            past = brow < qblk
            gate = jnp.where(past, _dot_nt(kmean, q), NEG)
            rank = _rank_desc(gate, 0, nblk)
            allow = (past & (rank < float(MOBA_TOPK))) | (brow == qblk)
            bias_t = jnp.where(allow, 0.0, NEG)
            bias_t = jnp.concatenate([bias_t, jnp.zeros((LANES - nblk, ch), F32)], axis=0)
            qaug_scr[rows, :HEAD_DIM] = q
            qaug_scr[rows, HEAD_DIM:] = bias_t.T.astype(BF16)

    q_aug = qaug_scr[pl.ds(pl.multiple_of(qi * t, t), t), :]

    def qk(j, n):
        return _dot_nt(q_aug, kaug_scr[pl.ds(pl.multiple_of(j * (n * t), n * t), n * t), :])

    def vt(j, n):
        return vaug_scr[pl.ds(pl.multiple_of(j * (n * t), n * t), n * t), :]

    def causal(s):
        r = lax.broadcasted_iota(jnp.int32, s.shape, 0)
        c = lax.broadcasted_iota(jnp.int32, s.shape, 1)
        return jnp.where(c <= r, s, NEG)

    o = _two_pass_attention(qk, vt, causal, qi, s_scr, m_scr, acc_scr)
    o_ref[...] = o.astype(BF16)


def _moba(proj, bsz, seq):
    nq = seq // MOBA_T
    nblk = seq // MOBA_BLOCK
    assert nblk % 8 == 0 and nblk <= LANES
    return pl.pallas_call(
        functools.partial(_moba_kernel, nblk=nblk),
        out_shape=jax.ShapeDtypeStruct((bsz * seq, MOBA_WIDTH), BF16),
        grid=(bsz, MOBA_HEADS, nq),
        in_specs=[
            pl.BlockSpec((seq, HEAD_DIM), lambda b, h, i: (b, CB_MQ + h)),
            pl.BlockSpec((seq, HEAD_DIM), lambda b, h, i: (b, CB_MK + h)),
            pl.BlockSpec((seq, HEAD_DIM), lambda b, h, i: (b, CB_MV + h)),
        ],
        out_specs=pl.BlockSpec((MOBA_T, HEAD_DIM), lambda b, h, i: (b * nq + i, h)),
        scratch_shapes=[
            pltpu.VMEM((nblk, HEAD_DIM), F32),
            pltpu.VMEM((seq, 2 * HEAD_DIM), BF16),
            pltpu.VMEM((seq, 2 * HEAD_DIM), BF16),
            pltpu.VMEM((seq, 2 * HEAD_DIM), BF16),
            pltpu.VMEM((nq, MOBA_T, MOBA_T), F32),
            pltpu.VMEM((MOBA_T, LANES), F32),
            pltpu.VMEM((MOBA_T, 2 * HEAD_DIM), F32),
        ],
        compiler_params=_cparams(("arbitrary", "arbitrary", "arbitrary")),
        name="moba",
    )(proj, proj, proj)


CMP_HALF_FEAT = CMP_STRIDE * HEAD_DIM


def _compress_kernel(x_ref, pos_ref, w1f_ref, w1_ref, w2_ref, o_ref):
    x = x_ref[0, 0]
    a = jnp.dot(x, w1_ref[:CMP_HALF_FEAT, :], preferred_element_type=F32)
    b = jnp.dot(x, w1_ref[CMP_HALF_FEAT:, :], preferred_element_type=F32)
    nch = x.shape[0]
    b_next = pltpu.roll(b, nch - 1, 0)
    pos_bias = jnp.sum(w1f_ref[...] * pos_ref[...], axis=0, keepdims=True)
    hid = jax.nn.gelu(a + b_next + pos_bias, approximate=True)
    o_ref[0, 0] = jnp.dot(hid.astype(BF16), w2_ref[...], preferred_element_type=F32).astype(BF16)


def _compress(xc, pos_col, w1, w2):
    bsz, hk, nch, feat = xc.shape
    return pl.pallas_call(
        _compress_kernel,
        out_shape=jax.ShapeDtypeStruct((bsz, hk, nch, HEAD_DIM), BF16),
        grid=(bsz, hk),
        in_specs=[
            pl.BlockSpec((1, 1, nch, feat), lambda b, k: (b, k, 0, 0)),
            pl.BlockSpec((CMP_LEN * HEAD_DIM, 1), lambda b, k: (0, 0)),
            pl.BlockSpec((CMP_LEN * HEAD_DIM, CMP_HIDDEN), lambda b, k: (0, 0)),
            pl.BlockSpec((CMP_LEN * HEAD_DIM, CMP_HIDDEN), lambda b, k: (0, 0)),
            pl.BlockSpec((CMP_HIDDEN, HEAD_DIM), lambda b, k: (0, 0)),
        ],
        out_specs=pl.BlockSpec((1, 1, nch, HEAD_DIM), lambda b, k: (b, k, 0, 0)),
        compiler_params=_cparams(("arbitrary", "arbitrary")),
        name="compress",
    )(xc, pos_col, w1, w1.astype(BF16), w2.astype(BF16))


NSA_TQ = 256
NSA_TK = 512
NSA_ROWS = NSA_GROUP * NSA_TQ


def _split3(x):
    hi = x.astype(BF16)
    r1 = x - hi.astype(F32)
    mid = r1.astype(BF16)
    lo = (r1 - mid.astype(F32)).astype(BF16)
    return hi, mid, lo


def _nsa_kernel(q_ref, kc_ref, vc_ref, ks_ref, vs_ref, kw_ref, vw_ref, gl_ref, mt_ref, wb_ref, o_ref,
                kaug_scr, vsaug_scr, vwaug_scr, s_scr, m_scr, acc_scr):
    qi = pl.program_id(2)
    tq = NSA_TQ
    tk = NSA_TK
    seq = ks_ref.shape[0]
    slc_shift = SLC_BLOCK.bit_length() - 1

    @pl.when(qi == 0)
    def _():
        kcol = lax.broadcasted_iota(jnp.int32, (tk, LANES), 1)
        krow = lax.broadcasted_iota(jnp.int32, (tk, LANES), 0)
        for c in range(seq // tk):
            rows = slice(c * tk, (c + 1) * tk)
            kaug_scr[rows, :HEAD_DIM] = ks_ref[rows, :]
            blk = jnp.right_shift(c * tk + krow, slc_shift)
            kaug_scr[rows, HEAD_DIM:] = jnp.where(kcol == blk, 1.0, 0.0).astype(BF16)
            vsaug_scr[rows, :] = _ones_augment(vs_ref[rows, :])
            vwaug_scr[rows, :] = _ones_augment(vw_ref[rows, :])

    q = q_ref[...]
    qg = jnp.concatenate([q[:, g * HEAD_DIM:(g + 1) * HEAD_DIM] for g in range(NSA_GROUP)], axis=0)
    ncmp = kc_ref.shape[2]

    def qpos_of(shape):
        return qi * tq + jnp.bitwise_and(lax.broadcasted_iota(jnp.int32, shape, 0), tq - 1)

    span = WINDOW + tq
    start = pl.multiple_of(jnp.maximum(qi * tq - WINDOW, 0), tq)
    s = _dot_nt(qg, kw_ref[pl.ds(start, span), :])
    s = (s.reshape(NSA_GROUP, tq, span) + wb_ref[...]).reshape(NSA_ROWS, span)
    o_win = _softmax_pv(s, vwaug_scr[pl.ds(start, span), :])

    colc = lax.broadcasted_iota(jnp.int32, (NSA_ROWS, ncmp), 1)
    cmask = colc * CMP_STRIDE + (CMP_LEN - 1) <= qpos_of((NSA_ROWS, ncmp))
    s = jnp.where(cmask, _dot_nt(qg, kc_ref[0, 0]), NEG)
    e = jnp.exp2((s - jnp.max(s, axis=-1, keepdims=True)) * EXP2_SCALE)
    has_blk = qpos_of((NSA_ROWS, 1)) >= CMP_LEN - 1
    p = e * jnp.where(has_blk, 1.0 / jnp.sum(e, axis=-1, keepdims=True), 0.0)
    o_cmp = jnp.dot(p.astype(BF16), vc_ref[0, 0], preferred_element_type=F32)

    imp = p[0:tq]
    for g in range(1, NSA_GROUP):
        imp = imp + p[g * tq:(g + 1) * tq]
    n_slc = seq // SLC_BLOCK
    mt = mt_ref[...]
    p_slc = sum(_dot_nt(mt, part) for part in _split3(imp))
    jrow = lax.broadcasted_iota(jnp.int32, p_slc.shape, 0)
    jt = jnp.right_shift(qi * tq + lax.broadcasted_iota(jnp.int32, p_slc.shape, 1), slc_shift)
    valid = jrow <= jt
    forced = (jrow == 0) | (jrow == jt) | (jrow == jt - 1)
    score = jnp.where(valid & forced, BIG, jnp.where(valid, p_slc, NEG))
    rank = _rank_desc(score, 0, n_slc)
    sel = valid & (rank < float(min(SLC_TOPK, n_slc)))
    bias_t = jnp.where(sel, 0.0, NEG)
    bias_t = jnp.concatenate([bias_t, jnp.zeros((LANES - n_slc, tq), F32)], axis=0)
    bias = bias_t.T.astype(BF16)
    q_aug = jnp.concatenate([qg, jnp.concatenate([bias] * NSA_GROUP, axis=0)], axis=1)

    def slc_qk(j, n):
        return _dot_nt(q_aug, kaug_scr[pl.ds(pl.multiple_of(j * (n * tk), n * tk), n * tk), :])

    def slc_v(j, n):
        return vsaug_scr[pl.ds(pl.multiple_of(j * (n * tk), n * tk), n * tk), :]

    n_full = (qi * tq) // tk

    def slc_causal(s):
        kpos = n_full * tk + lax.broadcasted_iota(jnp.int32, s.shape, 1)
        return jnp.where(kpos <= qpos_of(s.shape), s, NEG)

    o_slc = _two_pass_attention(slc_qk, slc_v, slc_causal, n_full, s_scr, m_scr, acc_scr)

    sig = jax.nn.sigmoid(gl_ref[...])
    kv_head = pl.program_id(1)

    def gate_col(branch, g):
        cols = [sig[:, branch * NSA_HEADS + k * NSA_GROUP + g:branch * NSA_HEADS + k * NSA_GROUP + g + 1]
                for k in range(NSA_KV_HEADS)]
        col = cols[0]
        for k in range(1, NSA_KV_HEADS):
            col = jnp.where(kv_head == k, cols[k], col)
        return col

    for g in range(NSA_GROUP):
        rows = slice(g * tq, (g + 1) * tq)
        og = (gate_col(0, g) * o_cmp[rows] + gate_col(1, g) * o_slc[rows] + gate_col(2, g) * o_win[rows])
        o_ref[:, g * HEAD_DIM:(g + 1) * HEAD_DIM] = og.astype(BF16)


def _slc_weight_matrix(n_slc, ncmp):
    rs, rc = SLC_BLOCK // CMP_STRIDE, CMP_LEN // CMP_STRIDE
    j = jnp.arange(n_slc)[:, None]
    n = jnp.arange(ncmp)[None, :]
    w = jnp.zeros((n_slc, ncmp), F32)
    for o in range(rs + rc - 1):
        w_o = float(sum(1 for m in range(rs) for nn in range(rc) if m - nn + rc - 1 == o))
        w = w + jnp.where(n == rs * j + o - (rc - 1), w_o, 0.0)
    return w.astype(BF16)


def _window_bias():
    span = WINDOW + NSA_TQ
    cases = WINDOW // NSA_TQ + 1
    c = jnp.arange(cases)[:, None, None]
    qpos = c * NSA_TQ + jnp.arange(NSA_TQ)[None, :, None]
    kpos = jnp.maximum(c * NSA_TQ - WINDOW, 0) + jnp.arange(span)[None, None, :]
    return jnp.where((kpos <= qpos) & (kpos > qpos - WINDOW), 0.0, NEG).astype(F32)


def _nsa(proj, gates, kc, vc, bsz, seq):
    nq = seq // NSA_TQ
    ncmp = kc.shape[2]
    n_slc = seq // SLC_BLOCK
    assert n_slc % 8 == 0 and n_slc <= LANES and seq % NSA_TK == 0
    wbias = _window_bias()
    last_case = wbias.shape[0] - 1
    qw = NSA_GROUP * HEAD_DIM
    kv_spec = lambda cb: pl.BlockSpec((seq, HEAD_DIM), lambda b, k, i: (b, cb + k))
    cmp_spec = pl.BlockSpec((1, 1, ncmp, HEAD_DIM), lambda b, k, i: (b, k, 0, 0))
    return pl.pallas_call(
        _nsa_kernel,
        out_shape=jax.ShapeDtypeStruct((bsz * seq, NSA_WIDTH), BF16),
        grid=(bsz, NSA_KV_HEADS, nq),
        in_specs=[
            pl.BlockSpec((NSA_TQ, qw), lambda b, k, i: (b * nq + i, CB_NQ * HEAD_DIM // qw + k)),
            cmp_spec, cmp_spec,
            kv_spec(CB_KS), kv_spec(CB_VS), kv_spec(CB_KW), kv_spec(CB_VW),
            pl.BlockSpec((NSA_TQ, LANES), lambda b, k, i: (b * nq + i, 0)),
            pl.BlockSpec((n_slc, ncmp), lambda b, k, i: (0, 0)),
            pl.BlockSpec((1,) + wbias.shape[1:], lambda b, k, i: (jnp.minimum(i, last_case), 0, 0)),
        ],
        out_specs=pl.BlockSpec((NSA_TQ, qw), lambda b, k, i: (b * nq + i, k)),
        scratch_shapes=[
            pltpu.VMEM((seq, 2 * HEAD_DIM), BF16),
            pltpu.VMEM((seq, 2 * HEAD_DIM), BF16),
            pltpu.VMEM((seq, 2 * HEAD_DIM), BF16),
            pltpu.VMEM((seq // NSA_TK, NSA_ROWS, NSA_TK), F32),
            pltpu.VMEM((NSA_ROWS, LANES), F32),
            pltpu.VMEM((NSA_ROWS, 2 * HEAD_DIM), F32),
        ],
        compiler_params=_cparams(("arbitrary", "arbitrary", "arbitrary")),
        name="nsa",
    )(proj, kc, vc, proj, proj, proj, proj, gates, _slc_weight_matrix(n_slc, ncmp), wbias)


OUT_TM = 512


def _outproj_kernel(om_ref, on_ref, w_ref, x_ref, ga_ref, npost_ref, npre_ref, sh_ref, sc_ref,
                    x1_ref, h2_ref):
    o = jnp.concatenate([om_ref[...], on_ref[...]], axis=1)
    y = jnp.dot(o, w_ref[...], preferred_element_type=F32)
    x1 = x_ref[...] + ga_ref[0] * _rms(y, npost_ref[...])
    x1_ref[...] = x1
    h2_ref[...] = (_rms(x1, npre_ref[...]) * (1.0 + sc_ref[0]) + sh_ref[0]).astype(BF16)


def _outproj(o_moba, o_nsa, w_out, x2d, mod3, npost, npre, seq):
    rows = x2d.shape[0]
    tpb = seq // OUT_TM
    mod_spec = lambda idx: pl.BlockSpec((1, 1, D_MODEL), lambda i: ((i // tpb) * 6 + idx, 0, 0))
    vec_spec = pl.BlockSpec((1, D_MODEL), lambda i: (0, 0))
    w_bf = w_out.astype(BF16)
    return pl.pallas_call(
        _outproj_kernel,
        out_shape=(jax.ShapeDtypeStruct((rows, D_MODEL), F32),
                   jax.ShapeDtypeStruct((rows, D_MODEL), BF16)),
        grid=(rows // OUT_TM,),
        in_specs=[
            pl.BlockSpec((OUT_TM, MOBA_WIDTH), lambda i: (i, 0)),
            pl.BlockSpec((OUT_TM, NSA_WIDTH), lambda i: (i, 0)),
            pl.BlockSpec((MOBA_WIDTH + NSA_WIDTH, D_MODEL), lambda i: (0, 0)),
            pl.BlockSpec((OUT_TM, D_MODEL), lambda i: (i, 0)),
            mod_spec(2), vec_spec, vec_spec, mod_spec(3), mod_spec(4),
        ],
        out_specs=(pl.BlockSpec((OUT_TM, D_MODEL), lambda i: (i, 0)),
                   pl.BlockSpec((OUT_TM, D_MODEL), lambda i: (i, 0))),
        compiler_params=_cparams(("arbitrary",)),
        name="outproj",
    )(o_moba, o_nsa, w_bf, x2d, mod3, npost, npre, mod3, mod3)


FFN_TM = 512
FFN_TF = 512
FFN_HALO = 16
CONV_WIDTH = 3


def _ffn_kernel(h_ref, halo_ref, wg_ref, wv_ref, cg_ref, cv_ref, wd_ref,
                x1_ref, gf_ref, npost_ref, o_ref, acc_scr, u_scr, haug_scr, *, tiles_per_seq):
    i = pl.program_id(0)
    j = pl.program_id(1)

    @pl.when(j == 0)
    def _():
        acc_scr[...] = jnp.zeros(acc_scr.shape, F32)
        halo = halo_ref[...]
        haug_scr[:FFN_HALO, :] = jnp.where(i % tiles_per_seq == 0, jnp.zeros_like(halo), halo)
        haug_scr[FFN_HALO:, :] = h_ref[...]

    h_aug = haug_scr[...]

    def conv_branch(w_ref, c_ref):
        u_scr[...] = jnp.dot(h_aug, w_ref[...], preferred_element_type=F32)
        c = c_ref[...]
        out = c[CONV_WIDTH:CONV_WIDTH + 1, :]
        for tap in range(CONV_WIDTH):
            ofs = FFN_HALO - (CONV_WIDTH - 1) + tap
            out = out + c[tap:tap + 1, :] * u_scr[pl.ds(ofs, FFN_TM), :]
        return out

    gate = conv_branch(wg_ref, cg_ref)
    val = conv_branch(wv_ref, cv_ref)
    act = (jax.nn.gelu(gate, approximate=True) * val).astype(BF16)
    acc_scr[...] += jnp.dot(act, wd_ref[...], preferred_element_type=F32)

    @pl.when(j == pl.num_programs(1) - 1)
    def _():
        o_ref[...] = x1_ref[...] + gf_ref[0] * _rms(acc_scr[...], npost_ref[...])


def _ffn(h2, x1, w_up, conv_w, conv_b, w_down, mod3, npost, seq):
    rows = h2.shape[0]
    tps = seq // FFN_TM
    nj = D_FF // FFN_TF
    w_up_bf = w_up.astype(BF16)
    conv_p = jnp.concatenate([conv_w, conv_b.reshape(1, 2 * D_FF)], axis=0)
    return pl.pallas_call(
        functools.partial(_ffn_kernel, tiles_per_seq=tps),
        out_shape=jax.ShapeDtypeStruct((rows, D_MODEL), F32),
        grid=(rows // FFN_TM, nj),
        in_specs=[
            pl.BlockSpec((FFN_TM, D_MODEL), lambda i, j: (i, 0)),
            pl.BlockSpec((FFN_HALO, D_MODEL),
                         lambda i, j: (jnp.maximum(i * (FFN_TM // FFN_HALO) - 1, 0), 0)),
            pl.BlockSpec((D_MODEL, FFN_TF), lambda i, j: (0, j)),
            pl.BlockSpec((D_MODEL, FFN_TF), lambda i, j: (0, nj + j)),
            pl.BlockSpec((CONV_WIDTH + 1, FFN_TF), lambda i, j: (0, j)),
            pl.BlockSpec((CONV_WIDTH + 1, FFN_TF), lambda i, j: (0, nj + j)),
            pl.BlockSpec((FFN_TF, D_MODEL), lambda i, j: (j, 0)),
            pl.BlockSpec((FFN_TM, D_MODEL), lambda i, j: (i, 0)),
            pl.BlockSpec((1, 1, D_MODEL), lambda i, j: ((i // tps) * 6 + 5, 0, 0)),
            pl.BlockSpec((1, D_MODEL), lambda i, j: (0, 0)),
        ],
        out_specs=pl.BlockSpec((FFN_TM, D_MODEL), lambda i, j: (i, 0)),
        scratch_shapes=[
            pltpu.VMEM((FFN_TM, D_MODEL), F32),
            pltpu.VMEM((FFN_HALO + FFN_TM, FFN_TF), F32),
            pltpu.VMEM((FFN_HALO + FFN_TM, D_MODEL), BF16),
        ],
        compiler_params=_cparams(("arbitrary", "arbitrary")),
        name="ffn",
    )(h2, h2, w_up_bf, w_up_bf, conv_p, conv_p, w_down.astype(BF16), x1, mod3, npost)


def _chunks(proj, cb, bsz, seq):
    a = proj[:, cb * HEAD_DIM:(cb + NSA_KV_HEADS) * HEAD_DIM]
    a = a.reshape(bsz, seq // CMP_STRIDE, CMP_STRIDE, NSA_KV_HEADS, HEAD_DIM).transpose(0, 3, 1, 2, 4)
    return a.reshape(bsz, NSA_KV_HEADS, seq // CMP_STRIDE, CMP_HALF_FEAT)


def kernel(x, c, positions, w_ada, b_ada, norm_pre_mix, norm_post_mix, norm_pre_ffn, norm_post_ffn,
           w_in, w_out, cmp_pos_k, cmp_w1_k, cmp_w2_k, cmp_pos_v, cmp_w1_v, cmp_w2_v,
           w_up, conv_w, conv_b, w_down):
    bsz, seq, d = x.shape
    depth = w_in.shape[0]
    assert d == D_MODEL and seq % WINDOW == 0 and seq % PROJ_TM == 0
    cos, sin = _rope_tables(positions)
    x2d = x.reshape(bsz * seq, d)
    for l in range(depth):
        mod3 = _ada(c, w_ada[l], b_ada[l]).reshape(bsz * 6, 1, d)
        assert w_in.shape[2] == PROJ_COLS + 3 * NSA_HEADS
        proj, gates = _inproj(x2d, norm_pre_mix[l].reshape(1, d), mod3, cos, sin, w_in[l], seq)
        o_moba = _moba(proj, bsz, seq)
        kc = _compress(_chunks(proj, CB_KC, bsz, seq), cmp_pos_k[l].reshape(-1, 1), cmp_w1_k[l], cmp_w2_k[l])
        vc = _compress(_chunks(proj, CB_VC, bsz, seq), cmp_pos_v[l].reshape(-1, 1), cmp_w1_v[l], cmp_w2_v[l])
        o_nsa = _nsa(proj, gates, kc, vc, bsz, seq)
        x1, h2 = _outproj(o_moba, o_nsa, w_out[l], x2d, mod3, norm_post_mix[l].reshape(1, d),
                          norm_pre_ffn[l].reshape(1, d), seq)
        x2d = _ffn(h2, x1, w_up[l], conv_w[l], conv_b[l], w_down[l], mod3,
                   norm_post_ffn[l].reshape(1, d), seq)
    return x2d.reshape(bsz, seq, d)
```

```python
import functools

import jax
import jax.numpy as jnp
from jax import lax
from jax.experimental import pallas as pl
from jax.experimental.pallas import tpu as pltpu

F32 = jnp.float32
BF16 = jnp.bfloat16

D_MODEL = 2048
HEAD_DIM = 128
HALF = HEAD_DIM // 2
MOBA_HEADS = 8
NSA_HEADS = 8
NSA_KV_HEADS = 2
NSA_GROUP = NSA_HEADS // NSA_KV_HEADS
MOBA_BLOCK = 256
MOBA_TOPK = 3
CMP_LEN = 32
CMP_STRIDE = 16
CMP_HIDDEN = 2 * HEAD_DIM
SLC_BLOCK = 64
SLC_TOPK = 16
WINDOW = 512
D_FF = 4 * D_MODEL
ROPE_THETA = 10000.0
EPS = 1e-6
NEG = -1e30
BIG = 1e9
SCALE = HEAD_DIM ** -0.5

MOBA_WIDTH = MOBA_HEADS * HEAD_DIM
NSA_WIDTH = NSA_HEADS * HEAD_DIM
KV_WIDTH = NSA_KV_HEADS * HEAD_DIM

LANES = 128
MXU_DIM = 256
VMEM_LIMIT = 56 * 1024 * 1024

CB_MQ, CB_MK, CB_MV, CB_NQ = 0, 8, 16, 24
CB_KC, CB_VC, CB_KS, CB_VS, CB_KW, CB_VW = 32, 34, 36, 38, 40, 42
PROJ_GROUPS = 44
PROJ_COLS = PROJ_GROUPS * HEAD_DIM
PROJ_COLS_PADDED = PROJ_COLS + LANES
ROPE_GROUPS = frozenset(list(range(CB_MQ, CB_MV)) + list(range(CB_NQ, CB_VC))
                        + [CB_KS, CB_KS + 1, CB_KW, CB_KW + 1])
assert PROJ_COLS % MXU_DIM == 0


def _cparams(sem):
    return pltpu.CompilerParams(dimension_semantics=sem, vmem_limit_bytes=VMEM_LIMIT)


def _dot_nt(a, b):
    return lax.dot_general(a, b, (((1,), (1,)), ((), ())), preferred_element_type=F32)


def _rms(x, w):
    ms = jnp.mean(x * x, axis=-1, keepdims=True)
    return (x * lax.rsqrt(ms + EPS)) * w


ADA_TN = 2048
ADA_TK = 1024


def _ada_kernel(ct_ref, w_ref, b_ref, o_ref):
    k = pl.program_id(1)

    @pl.when(k == 0)
    def _():
        o_ref[...] = jnp.broadcast_to(b_ref[...], o_ref.shape)

    ct = ct_ref[...]
    s = ct * jax.nn.sigmoid(ct)
    w = w_ref[...]
    rows = [jnp.sum(w * s[:, b:b + 1], axis=0, keepdims=True) for b in range(ct.shape[1])]
    o_ref[...] += jnp.concatenate(rows, axis=0)


def _ada(c, w_ada, b_ada):
    bsz = c.shape[0]
    n = w_ada.shape[1]
    return pl.pallas_call(
        _ada_kernel,
        out_shape=jax.ShapeDtypeStruct((bsz, n), F32),
        grid=(n // ADA_TN, D_MODEL // ADA_TK),
        in_specs=[
            pl.BlockSpec((ADA_TK, bsz), lambda j, k: (k, 0)),
            pl.BlockSpec((ADA_TK, ADA_TN), lambda j, k: (k, j)),
            pl.BlockSpec((1, ADA_TN), lambda j, k: (0, j)),
        ],
        out_specs=pl.BlockSpec((bsz, ADA_TN), lambda j, k: (0, j)),
        compiler_params=_cparams(("arbitrary", "arbitrary")),
        name="ada",
    )(c.T, w_ada, b_ada.reshape(1, n))


ROPE_TS = 1024


def _rope_kernel(pos_ref, inv_ref, cos_ref, sin_ref):
    half_rows = pos_ref.shape[0] // 2
    lane = lax.broadcasted_iota(jnp.int32, (half_rows, HEAD_DIM), 1)
    low = lane < HALF
    pos = jnp.where(low, pos_ref[:half_rows, :], pos_ref[half_rows:, :]).astype(F32)
    ang = pos * inv_ref[...]
    cs = jnp.cos(ang)
    sn = jnp.sin(ang)
    cs_sw = pltpu.roll(cs, HALF, 1)
    sn_sw = pltpu.roll(sn, HALF, 1)
    cos_ref[:half_rows, :] = jnp.where(low, cs, cs_sw)
    cos_ref[half_rows:, :] = jnp.where(low, cs_sw, cs)
    sin_ref[:half_rows, :] = jnp.where(low, -sn, sn_sw)
    sin_ref[half_rows:, :] = jnp.where(low, -sn_sw, sn)


def _rope_tables(positions):
    rows = positions.size
    inv = ROPE_THETA ** (-jnp.arange(HALF, dtype=F32) / HALF)
    inv = jnp.concatenate([inv, inv]).reshape(1, HEAD_DIM)
    return pl.pallas_call(
        _rope_kernel,
        out_shape=(jax.ShapeDtypeStruct((rows, HEAD_DIM), F32),) * 2,
        grid=(rows // ROPE_TS,),
        in_specs=[
            pl.BlockSpec((ROPE_TS, 1), lambda i: (i, 0)),
            pl.BlockSpec((1, HEAD_DIM), lambda i: (0, 0)),
        ],
        out_specs=(pl.BlockSpec((ROPE_TS, HEAD_DIM), lambda i: (i, 0)),) * 2,
        compiler_params=_cparams(("arbitrary",)),
        name="rope_tables",
    )(positions.reshape(rows, 1), inv)


PROJ_TM = 512


def _inproj_kernel(x_ref, nw_ref, sh_ref, sc_ref, cos_ref, sin_ref, w_ref, wgate_ref, o_ref, g_ref):
    h = (_rms(x_ref[...], nw_ref[...]) * (1.0 + sc_ref[0]) + sh_ref[0]).astype(BF16)
    cos = cos_ref[...]
    sin = sin_ref[...]
    for c in range(PROJ_COLS // MXU_DIM):
        acc = jnp.dot(h, w_ref[:, c * MXU_DIM:(c + 1) * MXU_DIM], preferred_element_type=F32)
        for g in range(MXU_DIM // HEAD_DIM):
            grp = c * (MXU_DIM // HEAD_DIM) + g
            y = acc[:, g * HEAD_DIM:(g + 1) * HEAD_DIM]
            if grp in ROPE_GROUPS:
                y = y * cos + pltpu.roll(y, HALF, 1) * sin
            o_ref[:, grp * HEAD_DIM:(grp + 1) * HEAD_DIM] = y.astype(BF16)
    g_ref[...] = jnp.dot(h, wgate_ref[...], preferred_element_type=F32)


def _inproj(x2d, nw, mod3, cos, sin, w_in, seq):
    rows = x2d.shape[0]
    tpb = seq // PROJ_TM
    w_main = w_in[:, :PROJ_COLS].astype(BF16)
    w_gate = jnp.pad(w_in[:, PROJ_COLS:], ((0, 0), (0, PROJ_COLS_PADDED - w_in.shape[1]))).astype(BF16)
    return pl.pallas_call(
        _inproj_kernel,
        out_shape=(jax.ShapeDtypeStruct((rows, PROJ_COLS), BF16),
                   jax.ShapeDtypeStruct((rows, LANES), F32)),
        grid=(rows // PROJ_TM,),
        in_specs=[
            pl.BlockSpec((PROJ_TM, D_MODEL), lambda i: (i, 0)),
            pl.BlockSpec((1, D_MODEL), lambda i: (0, 0)),
            pl.BlockSpec((1, 1, D_MODEL), lambda i: ((i // tpb) * 6 + 0, 0, 0)),
            pl.BlockSpec((1, 1, D_MODEL), lambda i: ((i // tpb) * 6 + 1, 0, 0)),
            pl.BlockSpec((PROJ_TM, HEAD_DIM), lambda i: (i, 0)),
            pl.BlockSpec((PROJ_TM, HEAD_DIM), lambda i: (i, 0)),
            pl.BlockSpec((D_MODEL, PROJ_COLS), lambda i: (0, 0), pipeline_mode=pl.Buffered(1)),
            pl.BlockSpec((D_MODEL, LANES), lambda i: (0, 0)),
        ],
        out_specs=(pl.BlockSpec((PROJ_TM, PROJ_COLS), lambda i: (i, 0)),
                   pl.BlockSpec((PROJ_TM, LANES), lambda i: (i, 0))),
        compiler_params=_cparams(("arbitrary",)),
        name="inproj",
    )(x2d, nw, mod3, mod3, cos, sin, w_main, w_gate)


EXP2_SCALE = SCALE * 1.4426950408889634


def _lane_tiles(x):
    return [x[:, t * LANES:(t + 1) * LANES] for t in range(x.shape[1] // LANES)]


def _lane_fold(x, op):
    return functools.reduce(op, _lane_tiles(x))


def _ones_augment(v):
    return jnp.concatenate([v, jnp.ones(v.shape, v.dtype)], axis=1)


def _normalize(pv):
    return pv[:, :HEAD_DIM] * (1.0 / pv[:, HEAD_DIM:HEAD_DIM + 1])


def _softmax_pv(s, v_aug):
    m = jnp.max(s, axis=-1, keepdims=True)
    p = jnp.exp2((s - m) * EXP2_SCALE)
    return _normalize(jnp.dot(p.astype(BF16), v_aug, preferred_element_type=F32))


def _two_pass_attention(qk_fn, v_fn, last_mask_fn, n_full, s_scr, m_scr, acc_scr):
    m_scr[...] = jnp.full(m_scr.shape, NEG, F32)

    def pass1_tile(j):
        s = qk_fn(j, 1)
        s_scr[j] = s
        return _lane_fold(s, jnp.maximum)

    def pass1(pair, carry):
        s = qk_fn(pair, 2)
        tk = s.shape[1] // 2
        s_scr[2 * pair] = s[:, :tk]
        s_scr[2 * pair + 1] = s[:, tk:]
        m_scr[...] = jnp.maximum(m_scr[...], _lane_fold(s, jnp.maximum))
        return carry

    lax.fori_loop(0, n_full // 2, pass1, 0)

    @pl.when(n_full % 2 == 1)
    def _():
        m_scr[...] = jnp.maximum(m_scr[...], pass1_tile(n_full - 1))

    s = last_mask_fn(qk_fn(n_full, 1))
    s_scr[n_full] = s
    m = jnp.max(jnp.maximum(m_scr[...], _lane_fold(s, jnp.maximum)), axis=-1, keepdims=True)
    m_scr[...] = jnp.broadcast_to(m, m_scr.shape)
    acc_scr[...] = jnp.zeros(acc_scr.shape, F32)

    def pass2_tile(j):
        mb = m_scr[...]
        ps = [jnp.exp2((st - mb) * EXP2_SCALE) for st in _lane_tiles(s_scr[j])]
        p = jnp.concatenate(ps, axis=1).astype(BF16)
        return jnp.dot(p, v_fn(j, 1), preferred_element_type=F32)

    def pass2(pair, carry):
        mb = m_scr[...]
        ps = [jnp.exp2((st - mb) * EXP2_SCALE)
              for t in (2 * pair, 2 * pair + 1) for st in _lane_tiles(s_scr[t])]
        p = jnp.concatenate(ps, axis=1).astype(BF16)
        acc_scr[...] += jnp.dot(p, v_fn(pair, 2), preferred_element_type=F32)
        return carry

    n_tiles = n_full + 1
    lax.fori_loop(0, n_tiles // 2, pass2, 0)

    @pl.when(n_tiles % 2 == 1)
    def _():
        acc_scr[...] += pass2_tile(n_tiles - 1)

    return _normalize(acc_scr[...])


def _rank_desc(score, axis, count):
    idx = lax.broadcasted_iota(jnp.int32, score.shape, axis)
    rank = jnp.zeros(score.shape, F32)
    for jp in range(count):
        other = score[:, jp:jp + 1] if axis == 1 else score[jp:jp + 1, :]
        beats = (other > score) | ((other == score) & (idx > jp))
        rank = rank + jnp.where(beats, 1.0, 0.0)
    return rank


MOBA_T = 1024
MOBA_SETUP_CH = 512


def _moba_kernel(q_ref, k_ref, v_ref, wcast_in_ref, o_ref, wcast_out_ref,
                 kmean_scr, qaug_scr, kaug_scr, vaug_scr, s_scr, m_scr, acc_scr, *, nblk):
    wcast_out_ref[...] = wcast_in_ref[...].astype(BF16)
    qi = pl.program_id(2)
    t = MOBA_T
    blk_shift = MOBA_BLOCK.bit_length() - 1

    @pl.when(qi == 0)
    def _():
        ch = MOBA_SETUP_CH
        kcol = lax.broadcasted_iota(jnp.int32, (ch, LANES), 1)
        krow = lax.broadcasted_iota(jnp.int32, (ch, LANES), 0)
        for n in range(nblk):
            kb = k_ref[n * MOBA_BLOCK:(n + 1) * MOBA_BLOCK, :].astype(F32)
            kmean_scr[n:n + 1, :] = jnp.mean(kb, axis=0, keepdims=True)
        kmean = kmean_scr[...].astype(BF16)
        brow = lax.broadcasted_iota(jnp.int32, (nblk, ch), 0)
        qlane = lax.broadcasted_iota(jnp.int32, (nblk, ch), 1)
        for c in range(k_ref.shape[0] // ch):
            rows = slice(c * ch, (c + 1) * ch)
            kaug_scr[rows, :HEAD_DIM] = k_ref[rows, :]
            blk = jnp.right_shift(c * ch + krow, blk_shift)
            kaug_scr[rows, HEAD_DIM:] = jnp.where(kcol == blk, 1.0, 0.0).astype(BF16)
            vaug_scr[rows, :] = _ones_augment(v_ref[rows, :])
            q = q_ref[rows, :]
            qblk = jnp.right_shift(c * ch + qlane, blk_shift)
            past = brow < qblk
            gate = jnp.where(past, _dot_nt(kmean, q), NEG)
            rank = _rank_desc(gate, 0, nblk)
            allow = (past & (rank < float(MOBA_TOPK))) | (brow == qblk)
            bias_t = jnp.where(allow, 0.0, NEG)
            bias_t = jnp.concatenate([bias_t, jnp.zeros((LANES - nblk, ch), F32)], axis=0)
            qaug_scr[rows, :HEAD_DIM] = q
            qaug_scr[rows, HEAD_DIM:] = bias_t.T.astype(BF16)

    q_aug = qaug_scr[pl.ds(pl.multiple_of(qi * t, t), t), :]

    def qk(j, n):
        return _dot_nt(q_aug, kaug_scr[pl.ds(pl.multiple_of(j * (n * t), n * t), n * t), :])

    def vt(j, n):
        return vaug_scr[pl.ds(pl.multiple_of(j * (n * t), n * t), n * t), :]

    def causal(s):
        r = lax.broadcasted_iota(jnp.int32, s.shape, 0)
        c = lax.broadcasted_iota(jnp.int32, s.shape, 1)
        return jnp.where(c <= r, s, NEG)

    o = _two_pass_attention(qk, vt, causal, qi, s_scr, m_scr, acc_scr)
    o_ref[...] = o.astype(BF16)


def _moba(proj, w_cast, bsz, seq):
    nq = seq // MOBA_T
    nblk = seq // MOBA_BLOCK
    assert nblk % 8 == 0 and nblk <= LANES
    steps = bsz * MOBA_HEADS * nq
    wrows, wcols = w_cast.shape
    wc = wcols // steps
    assert wc * steps == wcols and wc % LANES == 0
    wspec = pl.BlockSpec((wrows, wc), lambda b, h, i: (0, (b * MOBA_HEADS + h) * nq + i))
    return pl.pallas_call(
        functools.partial(_moba_kernel, nblk=nblk),
        out_shape=(jax.ShapeDtypeStruct((bsz * seq, MOBA_WIDTH), BF16),
                   jax.ShapeDtypeStruct(w_cast.shape, BF16)),
        grid=(bsz, MOBA_HEADS, nq),
        in_specs=[
            pl.BlockSpec((seq, HEAD_DIM), lambda b, h, i: (b, CB_MQ + h)),
            pl.BlockSpec((seq, HEAD_DIM), lambda b, h, i: (b, CB_MK + h)),
            pl.BlockSpec((seq, HEAD_DIM), lambda b, h, i: (b, CB_MV + h)),
            wspec,
        ],
        out_specs=(pl.BlockSpec((MOBA_T, HEAD_DIM), lambda b, h, i: (b * nq + i, h)), wspec),
        scratch_shapes=[
            pltpu.VMEM((nblk, HEAD_DIM), F32),
            pltpu.VMEM((seq, 2 * HEAD_DIM), BF16),
            pltpu.VMEM((seq, 2 * HEAD_DIM), BF16),
            pltpu.VMEM((seq, 2 * HEAD_DIM), BF16),
            pltpu.VMEM((nq, MOBA_T, MOBA_T), F32),
            pltpu.VMEM((MOBA_T, LANES), F32),
            pltpu.VMEM((MOBA_T, 2 * HEAD_DIM), F32),
        ],
        compiler_params=_cparams(("arbitrary", "arbitrary", "arbitrary")),
        name="moba",
    )(proj, proj, proj, w_cast)


CMP_HALF_FEAT = CMP_STRIDE * HEAD_DIM


def _compress_kernel(x_ref, pos_ref, w1f_ref, w1_ref, w2_ref, o_ref):
    x = x_ref[0, 0]
    a = jnp.dot(x, w1_ref[:CMP_HALF_FEAT, :], preferred_element_type=F32)
    b = jnp.dot(x, w1_ref[CMP_HALF_FEAT:, :], preferred_element_type=F32)
    nch = x.shape[0]
    b_next = pltpu.roll(b, nch - 1, 0)
    pos_bias = jnp.sum(w1f_ref[...] * pos_ref[...], axis=0, keepdims=True)
    hid = jax.nn.gelu(a + b_next + pos_bias, approximate=True)
    o_ref[0, 0] = jnp.dot(hid.astype(BF16), w2_ref[...], preferred_element_type=F32).astype(BF16)


def _compress(xc, pos_col, w1, w2):
    bsz, hk, nch, feat = xc.shape
    return pl.pallas_call(
        _compress_kernel,
        out_shape=jax.ShapeDtypeStruct((bsz, hk, nch, HEAD_DIM), BF16),
        grid=(bsz, hk),
        in_specs=[
            pl.BlockSpec((1, 1, nch, feat), lambda b, k: (b, k, 0, 0)),
            pl.BlockSpec((CMP_LEN * HEAD_DIM, 1), lambda b, k: (0, 0)),
            pl.BlockSpec((CMP_LEN * HEAD_DIM, CMP_HIDDEN), lambda b, k: (0, 0)),
            pl.BlockSpec((CMP_LEN * HEAD_DIM, CMP_HIDDEN), lambda b, k: (0, 0)),
            pl.BlockSpec((CMP_HIDDEN, HEAD_DIM), lambda b, k: (0, 0)),
        ],
        out_specs=pl.BlockSpec((1, 1, nch, HEAD_DIM), lambda b, k: (b, k, 0, 0)),
        compiler_params=_cparams(("arbitrary", "arbitrary")),
        name="compress",
    )(xc, pos_col, w1, w1.astype(BF16), w2.astype(BF16))


NSA_TQ = 256
NSA_TK = 512
NSA_ROWS = NSA_GROUP * NSA_TQ


def _split3(x):
    hi = x.astype(BF16)
    r1 = x - hi.astype(F32)
    mid = r1.astype(BF16)
    lo = (r1 - mid.astype(F32)).astype(BF16)
    return hi, mid, lo


def _nsa_kernel(q_ref, kc_ref, vc_ref, ks_ref, vs_ref, kw_ref, vw_ref, gl_ref, mt_ref, wb_ref,
                wcast_in_ref, o_ref, wcast_out_ref,
                kaug_scr, vsaug_scr, vwaug_scr, s_scr, m_scr, acc_scr):
    wcast_out_ref[...] = wcast_in_ref[...].astype(BF16)
    qi = pl.program_id(2)
    tq = NSA_TQ
    tk = NSA_TK
    seq = ks_ref.shape[0]
    slc_shift = SLC_BLOCK.bit_length() - 1

    @pl.when(qi == 0)
    def _():
        kcol = lax.broadcasted_iota(jnp.int32, (tk, LANES), 1)
        krow = lax.broadcasted_iota(jnp.int32, (tk, LANES), 0)
        for c in range(seq // tk):
            rows = slice(c * tk, (c + 1) * tk)
            kaug_scr[rows, :HEAD_DIM] = ks_ref[rows, :]
            blk = jnp.right_shift(c * tk + krow, slc_shift)
            kaug_scr[rows, HEAD_DIM:] = jnp.where(kcol == blk, 1.0, 0.0).astype(BF16)
            vsaug_scr[rows, :] = _ones_augment(vs_ref[rows, :])
            vwaug_scr[rows, :] = _ones_augment(vw_ref[rows, :])

    q = q_ref[...]
    qg = jnp.concatenate([q[:, g * HEAD_DIM:(g + 1) * HEAD_DIM] for g in range(NSA_GROUP)], axis=0)
    ncmp = kc_ref.shape[2]

    def qpos_of(shape):
        return qi * tq + jnp.bitwise_and(lax.broadcasted_iota(jnp.int32, shape, 0), tq - 1)

    span = WINDOW + tq
    start = pl.multiple_of(jnp.maximum(qi * tq - WINDOW, 0), tq)
    s = _dot_nt(qg, kw_ref[pl.ds(start, span), :])
    s = (s.reshape(NSA_GROUP, tq, span) + wb_ref[...]).reshape(NSA_ROWS, span)
    o_win = _softmax_pv(s, vwaug_scr[pl.ds(start, span), :])

    colc = lax.broadcasted_iota(jnp.int32, (NSA_ROWS, ncmp), 1)
    cmask = colc * CMP_STRIDE + (CMP_LEN - 1) <= qpos_of((NSA_ROWS, ncmp))
    s = jnp.where(cmask, _dot_nt(qg, kc_ref[0, 0]), NEG)
    e = jnp.exp2((s - jnp.max(s, axis=-1, keepdims=True)) * EXP2_SCALE)
    has_blk = qpos_of((NSA_ROWS, 1)) >= CMP_LEN - 1
    p = e * jnp.where(has_blk, 1.0 / jnp.sum(e, axis=-1, keepdims=True), 0.0)
    o_cmp = jnp.dot(p.astype(BF16), vc_ref[0, 0], preferred_element_type=F32)

    imp = p[0:tq]
    for g in range(1, NSA_GROUP):
        imp = imp + p[g * tq:(g + 1) * tq]
    n_slc = seq // SLC_BLOCK
    mt = mt_ref[...]
    p_slc = sum(_dot_nt(mt, part) for part in _split3(imp))
    jrow = lax.broadcasted_iota(jnp.int32, p_slc.shape, 0)
    jt = jnp.right_shift(qi * tq + lax.broadcasted_iota(jnp.int32, p_slc.shape, 1), slc_shift)
    valid = jrow <= jt
    forced = (jrow == 0) | (jrow == jt) | (jrow == jt - 1)
    score = jnp.where(valid & forced, BIG, jnp.where(valid, p_slc, NEG))
    rank = _rank_desc(score, 0, n_slc)
    sel = valid & (rank < float(min(SLC_TOPK, n_slc)))
    bias_t = jnp.where(sel, 0.0, NEG)
    bias_t = jnp.concatenate([bias_t, jnp.zeros((LANES - n_slc, tq), F32)], axis=0)
    bias = bias_t.T.astype(BF16)
    q_aug = jnp.concatenate([qg, jnp.concatenate([bias] * NSA_GROUP, axis=0)], axis=1)

    def slc_qk(j, n):
        return _dot_nt(q_aug, kaug_scr[pl.ds(pl.multiple_of(j * (n * tk), n * tk), n * tk), :])

    def slc_v(j, n):
        return vsaug_scr[pl.ds(pl.multiple_of(j * (n * tk), n * tk), n * tk), :]

    n_full = (qi * tq) // tk

    def slc_causal(s):
        kpos = n_full * tk + lax.broadcasted_iota(jnp.int32, s.shape, 1)
        return jnp.where(kpos <= qpos_of(s.shape), s, NEG)

    o_slc = _two_pass_attention(slc_qk, slc_v, slc_causal, n_full, s_scr, m_scr, acc_scr)

    sig = jax.nn.sigmoid(gl_ref[...])
    kv_head = pl.program_id(1)

    def gate_col(branch, g):
        cols = [sig[:, branch * NSA_HEADS + k * NSA_GROUP + g:branch * NSA_HEADS + k * NSA_GROUP + g + 1]
                for k in range(NSA_KV_HEADS)]
        col = cols[0]
        for k in range(1, NSA_KV_HEADS):
            col = jnp.where(kv_head == k, cols[k], col)
        return col

    for g in range(NSA_GROUP):
        rows = slice(g * tq, (g + 1) * tq)
        og = (gate_col(0, g) * o_cmp[rows] + gate_col(1, g) * o_slc[rows] + gate_col(2, g) * o_win[rows])
        o_ref[:, g * HEAD_DIM:(g + 1) * HEAD_DIM] = og.astype(BF16)


def _slc_weight_matrix(n_slc, ncmp):
    rs, rc = SLC_BLOCK // CMP_STRIDE, CMP_LEN // CMP_STRIDE
    j = jnp.arange(n_slc)[:, None]
    n = jnp.arange(ncmp)[None, :]
    w = jnp.zeros((n_slc, ncmp), F32)
    for o in range(rs + rc - 1):
        w_o = float(sum(1 for m in range(rs) for nn in range(rc) if m - nn + rc - 1 == o))
        w = w + jnp.where(n == rs * j + o - (rc - 1), w_o, 0.0)
    return w.astype(BF16)


def _window_bias():
    span = WINDOW + NSA_TQ
    cases = WINDOW // NSA_TQ + 1
    c = jnp.arange(cases)[:, None, None]
    qpos = c * NSA_TQ + jnp.arange(NSA_TQ)[None, :, None]
    kpos = jnp.maximum(c * NSA_TQ - WINDOW, 0) + jnp.arange(span)[None, None, :]
    return jnp.where((kpos <= qpos) & (kpos > qpos - WINDOW), 0.0, NEG).astype(F32)


def _nsa(proj, gates, kc, vc, w_cast, bsz, seq):
    nq = seq // NSA_TQ
    ncmp = kc.shape[2]
    n_slc = seq // SLC_BLOCK
    assert n_slc % 8 == 0 and n_slc <= LANES and seq % NSA_TK == 0
    wbias = _window_bias()
    last_case = wbias.shape[0] - 1
    qw = NSA_GROUP * HEAD_DIM
    kv_spec = lambda cb: pl.BlockSpec((seq, HEAD_DIM), lambda b, k, i: (b, cb + k))
    cmp_spec = pl.BlockSpec((1, 1, ncmp, HEAD_DIM), lambda b, k, i: (b, k, 0, 0))
    steps = bsz * NSA_KV_HEADS * nq
    wrows, wcols = w_cast.shape
    wr = wrows // steps
    assert wr * steps == wrows and wr % 16 == 0
    wspec = pl.BlockSpec((wr, wcols), lambda b, k, i: ((b * NSA_KV_HEADS + k) * nq + i, 0))
    return pl.pallas_call(
        _nsa_kernel,
        out_shape=(jax.ShapeDtypeStruct((bsz * seq, NSA_WIDTH), BF16),
                   jax.ShapeDtypeStruct(w_cast.shape, BF16)),
        grid=(bsz, NSA_KV_HEADS, nq),
        in_specs=[
            pl.BlockSpec((NSA_TQ, qw), lambda b, k, i: (b * nq + i, CB_NQ * HEAD_DIM // qw + k)),
            cmp_spec, cmp_spec,
            kv_spec(CB_KS), kv_spec(CB_VS), kv_spec(CB_KW), kv_spec(CB_VW),
            pl.BlockSpec((NSA_TQ, LANES), lambda b, k, i: (b * nq + i, 0)),
            pl.BlockSpec((n_slc, ncmp), lambda b, k, i: (0, 0)),
            pl.BlockSpec((1,) + wbias.shape[1:], lambda b, k, i: (jnp.minimum(i, last_case), 0, 0)),
            wspec,
        ],
        out_specs=(pl.BlockSpec((NSA_TQ, qw), lambda b, k, i: (b * nq + i, k)), wspec),
        scratch_shapes=[
            pltpu.VMEM((seq, 2 * HEAD_DIM), BF16),
            pltpu.VMEM((seq, 2 * HEAD_DIM), BF16),
            pltpu.VMEM((seq, 2 * HEAD_DIM), BF16),
            pltpu.VMEM((seq // NSA_TK, NSA_ROWS, NSA_TK), F32),
            pltpu.VMEM((NSA_ROWS, LANES), F32),
            pltpu.VMEM((NSA_ROWS, 2 * HEAD_DIM), F32),
        ],
        compiler_params=_cparams(("arbitrary", "arbitrary", "arbitrary")),
        name="nsa",
    )(proj, kc, vc, proj, proj, proj, proj, gates, _slc_weight_matrix(n_slc, ncmp), wbias, w_cast)


OUT_TM = 512


def _outproj_kernel(om_ref, on_ref, w_ref, x_ref, ga_ref, npost_ref, npre_ref, sh_ref, sc_ref,
                    x1_ref, h2_ref):
    o = jnp.concatenate([om_ref[...], on_ref[...]], axis=1)
    y = jnp.dot(o, w_ref[...], preferred_element_type=F32)
    x1 = x_ref[...] + ga_ref[0] * _rms(y, npost_ref[...])
    x1_ref[...] = x1
    h2_ref[...] = (_rms(x1, npre_ref[...]) * (1.0 + sc_ref[0]) + sh_ref[0]).astype(BF16)


def _outproj(o_moba, o_nsa, w_out, x2d, mod3, npost, npre, seq):
    rows = x2d.shape[0]
    tpb = seq // OUT_TM
    mod_spec = lambda idx: pl.BlockSpec((1, 1, D_MODEL), lambda i: ((i // tpb) * 6 + idx, 0, 0))
    vec_spec = pl.BlockSpec((1, D_MODEL), lambda i: (0, 0))
    w_bf = w_out.astype(BF16)
    return pl.pallas_call(
        _outproj_kernel,
        out_shape=(jax.ShapeDtypeStruct((rows, D_MODEL), F32),
                   jax.ShapeDtypeStruct((rows, D_MODEL), BF16)),
        grid=(rows // OUT_TM,),
        in_specs=[
            pl.BlockSpec((OUT_TM, MOBA_WIDTH), lambda i: (i, 0)),
            pl.BlockSpec((OUT_TM, NSA_WIDTH), lambda i: (i, 0)),
            pl.BlockSpec((MOBA_WIDTH + NSA_WIDTH, D_MODEL), lambda i: (0, 0)),
            pl.BlockSpec((OUT_TM, D_MODEL), lambda i: (i, 0)),
            mod_spec(2), vec_spec, vec_spec, mod_spec(3), mod_spec(4),
        ],
        out_specs=(pl.BlockSpec((OUT_TM, D_MODEL), lambda i: (i, 0)),
                   pl.BlockSpec((OUT_TM, D_MODEL), lambda i: (i, 0))),
        compiler_params=_cparams(("arbitrary",)),
        name="outproj",
    )(o_moba, o_nsa, w_bf, x2d, mod3, npost, npre, mod3, mod3)


FFN_TM = 512
FFN_TF = 512
FFN_HALO = 16
CONV_WIDTH = 3


def _ffn_kernel(h_ref, halo_ref, wg_ref, wv_ref, cg_ref, cv_ref, wd_ref,
                x1_ref, gf_ref, npost_ref, o_ref, acc_scr, u_scr, haug_scr, *, tiles_per_seq):
    i = pl.program_id(0)
    j = pl.program_id(1)

    @pl.when(j == 0)
    def _():
        acc_scr[...] = jnp.zeros(acc_scr.shape, F32)
        halo = halo_ref[...]
        haug_scr[:FFN_HALO, :] = jnp.where(i % tiles_per_seq == 0, jnp.zeros_like(halo), halo)
        haug_scr[FFN_HALO:, :] = h_ref[...]

    h_aug = haug_scr[...]

    def conv_branch(w_ref, c_ref):
        u_scr[...] = jnp.dot(h_aug, w_ref[...], preferred_element_type=F32)
        c = c_ref[...]
        out = c[CONV_WIDTH:CONV_WIDTH + 1, :]
        for tap in range(CONV_WIDTH):
            ofs = FFN_HALO - (CONV_WIDTH - 1) + tap
            out = out + c[tap:tap + 1, :] * u_scr[pl.ds(ofs, FFN_TM), :]
        return out

    gate = conv_branch(wg_ref, cg_ref)
    val = conv_branch(wv_ref, cv_ref)
    act = (jax.nn.gelu(gate, approximate=True) * val).astype(BF16)
    acc_scr[...] += jnp.dot(act, wd_ref[...], preferred_element_type=F32)

    @pl.when(j == pl.num_programs(1) - 1)
    def _():
        o_ref[...] = x1_ref[...] + gf_ref[0] * _rms(acc_scr[...], npost_ref[...])


def _ffn(h2, x1, w_up_bf, conv_w, conv_b, w_down_bf, mod3, npost, seq):
    rows = h2.shape[0]
    tps = seq // FFN_TM
    nj = D_FF // FFN_TF
    conv_p = jnp.concatenate([conv_w, conv_b.reshape(1, 2 * D_FF)], axis=0)
    return pl.pallas_call(
        functools.partial(_ffn_kernel, tiles_per_seq=tps),
        out_shape=jax.ShapeDtypeStruct((rows, D_MODEL), F32),
        grid=(rows // FFN_TM, nj),
        in_specs=[
            pl.BlockSpec((FFN_TM, D_MODEL), lambda i, j: (i, 0)),
            pl.BlockSpec((FFN_HALO, D_MODEL),
                         lambda i, j: (jnp.maximum(i * (FFN_TM // FFN_HALO) - 1, 0), 0)),
            pl.BlockSpec((D_MODEL, FFN_TF), lambda i, j: (0, j)),
            pl.BlockSpec((D_MODEL, FFN_TF), lambda i, j: (0, nj + j)),
            pl.BlockSpec((CONV_WIDTH + 1, FFN_TF), lambda i, j: (0, j)),
            pl.BlockSpec((CONV_WIDTH + 1, FFN_TF), lambda i, j: (0, nj + j)),
            pl.BlockSpec((FFN_TF, D_MODEL), lambda i, j: (j, 0)),
            pl.BlockSpec((FFN_TM, D_MODEL), lambda i, j: (i, 0)),
            pl.BlockSpec((1, 1, D_MODEL), lambda i, j: ((i // tps) * 6 + 5, 0, 0)),
            pl.BlockSpec((1, D_MODEL), lambda i, j: (0, 0)),
        ],
        out_specs=pl.BlockSpec((FFN_TM, D_MODEL), lambda i, j: (i, 0)),
        scratch_shapes=[
            pltpu.VMEM((FFN_TM, D_MODEL), F32),
            pltpu.VMEM((FFN_HALO + FFN_TM, FFN_TF), F32),
            pltpu.VMEM((FFN_HALO + FFN_TM, D_MODEL), BF16),
        ],
        compiler_params=_cparams(("arbitrary", "arbitrary")),
        name="ffn",
    )(h2, h2, w_up_bf, w_up_bf, conv_p, conv_p, w_down_bf, x1, mod3, npost)


def _chunks(proj, cb, bsz, seq):
    a = proj[:, cb * HEAD_DIM:(cb + NSA_KV_HEADS) * HEAD_DIM]
    a = a.reshape(bsz, seq // CMP_STRIDE, CMP_STRIDE, NSA_KV_HEADS, HEAD_DIM).transpose(0, 3, 1, 2, 4)
    return a.reshape(bsz, NSA_KV_HEADS, seq // CMP_STRIDE, CMP_HALF_FEAT)


def kernel(x, c, positions, w_ada, b_ada, norm_pre_mix, norm_post_mix, norm_pre_ffn, norm_post_ffn,
           w_in, w_out, cmp_pos_k, cmp_w1_k, cmp_w2_k, cmp_pos_v, cmp_w1_v, cmp_w2_v,
           w_up, conv_w, conv_b, w_down):
    bsz, seq, d = x.shape
    depth = w_in.shape[0]
    assert d == D_MODEL and seq % WINDOW == 0 and seq % PROJ_TM == 0
    cos, sin = _rope_tables(positions)
    x2d = x.reshape(bsz * seq, d)
    for l in range(depth):
        mod3 = _ada(c, w_ada[l], b_ada[l]).reshape(bsz * 6, 1, d)
        assert w_in.shape[2] == PROJ_COLS + 3 * NSA_HEADS
        proj, gates = _inproj(x2d, norm_pre_mix[l].reshape(1, d), mod3, cos, sin, w_in[l], seq)
        o_moba, w_up_bf = _moba(proj, w_up[l], bsz, seq)
        kc = _compress(_chunks(proj, CB_KC, bsz, seq), cmp_pos_k[l].reshape(-1, 1), cmp_w1_k[l], cmp_w2_k[l])
        vc = _compress(_chunks(proj, CB_VC, bsz, seq), cmp_pos_v[l].reshape(-1, 1), cmp_w1_v[l], cmp_w2_v[l])
        o_nsa, w_down_bf = _nsa(proj, gates, kc, vc, w_down[l], bsz, seq)
        x1, h2 = _outproj(o_moba, o_nsa, w_out[l], x2d, mod3, norm_post_mix[l].reshape(1, d),
                          norm_pre_ffn[l].reshape(1, d), seq)
        x2d = _ffn(h2, x1, w_up_bf, conv_w[l], conv_b[l], w_down_bf, mod3,
                   norm_post_ffn[l].reshape(1, d), seq)
    return x2d.reshape(bsz, seq, d)
```

```python
import functools

import jax
import jax.numpy as jnp
from jax import lax
from jax.experimental import pallas as pl
from jax.experimental.pallas import tpu as pltpu

F32 = jnp.float32
BF16 = jnp.bfloat16

D_MODEL = 2048
HEAD_DIM = 128
HALF = HEAD_DIM // 2
MOBA_HEADS = 8
NSA_HEADS = 8
NSA_KV_HEADS = 2
NSA_GROUP = NSA_HEADS // NSA_KV_HEADS
MOBA_BLOCK = 256
MOBA_TOPK = 3
CMP_LEN = 32
CMP_STRIDE = 16
CMP_HIDDEN = 2 * HEAD_DIM
SLC_BLOCK = 64
SLC_TOPK = 16
WINDOW = 512
D_FF = 4 * D_MODEL
ROPE_THETA = 10000.0
EPS = 1e-6
NEG = -1e30
BIG = 1e9
SCALE = HEAD_DIM ** -0.5

MOBA_WIDTH = MOBA_HEADS * HEAD_DIM
NSA_WIDTH = NSA_HEADS * HEAD_DIM
KV_WIDTH = NSA_KV_HEADS * HEAD_DIM

LANES = 128
MXU_DIM = 256
VMEM_LIMIT = 56 * 1024 * 1024

CB_MQ, CB_MK, CB_MV, CB_NQ = 0, 8, 16, 24
CB_KC, CB_VC, CB_KS, CB_VS, CB_KW, CB_VW = 32, 34, 36, 38, 40, 42
PROJ_GROUPS = 44
PROJ_COLS = PROJ_GROUPS * HEAD_DIM
PROJ_COLS_PADDED = PROJ_COLS + LANES
ROPE_GROUPS = frozenset(list(range(CB_MQ, CB_MV)) + list(range(CB_NQ, CB_VC))
                        + [CB_KS, CB_KS + 1, CB_KW, CB_KW + 1])
assert PROJ_COLS % MXU_DIM == 0


def _cparams(sem):
    return pltpu.CompilerParams(dimension_semantics=sem, vmem_limit_bytes=VMEM_LIMIT)


def _dot_nt(a, b):
    return lax.dot_general(a, b, (((1,), (1,)), ((), ())), preferred_element_type=F32)


def _rms(x, w):
    ms = jnp.mean(x * x, axis=-1, keepdims=True)
    return (x * lax.rsqrt(ms + EPS)) * w


ADA_TN = 2048
ADA_TK = 1024


def _ada_kernel(ct_ref, w_ref, b_ref, o_ref):
    k = pl.program_id(1)

    @pl.when(k == 0)
    def _():
        o_ref[...] = jnp.broadcast_to(b_ref[...], o_ref.shape)

    ct = ct_ref[...]
    s = ct * jax.nn.sigmoid(ct)
    w = w_ref[...]
    rows = [jnp.sum(w * s[:, b:b + 1], axis=0, keepdims=True) for b in range(ct.shape[1])]
    o_ref[...] += jnp.concatenate(rows, axis=0)


def _ada(c, w_ada, b_ada):
    bsz = c.shape[0]
    n = w_ada.shape[1]
    return pl.pallas_call(
        _ada_kernel,
        out_shape=jax.ShapeDtypeStruct((bsz, n), F32),
        grid=(n // ADA_TN, D_MODEL // ADA_TK),
        in_specs=[
            pl.BlockSpec((ADA_TK, bsz), lambda j, k: (k, 0)),
            pl.BlockSpec((ADA_TK, ADA_TN), lambda j, k: (k, j)),
            pl.BlockSpec((1, ADA_TN), lambda j, k: (0, j)),
        ],
        out_specs=pl.BlockSpec((bsz, ADA_TN), lambda j, k: (0, j)),
        compiler_params=_cparams(("arbitrary", "arbitrary")),
        name="ada",
    )(c.T, w_ada, b_ada.reshape(1, n))


ROPE_TS = 1024


def _rope_kernel(pos_ref, inv_ref, cos_ref, sin_ref):
    half_rows = pos_ref.shape[0] // 2
    lane = lax.broadcasted_iota(jnp.int32, (half_rows, HEAD_DIM), 1)
    low = lane < HALF
    pos = jnp.where(low, pos_ref[:half_rows, :], pos_ref[half_rows:, :]).astype(F32)
    ang = pos * inv_ref[...]
    cs = jnp.cos(ang)
    sn = jnp.sin(ang)
    cs_sw = pltpu.roll(cs, HALF, 1)
    sn_sw = pltpu.roll(sn, HALF, 1)
    cos_ref[:half_rows, :] = jnp.where(low, cs, cs_sw)
    cos_ref[half_rows:, :] = jnp.where(low, cs_sw, cs)
    sin_ref[:half_rows, :] = jnp.where(low, -sn, sn_sw)
    sin_ref[half_rows:, :] = jnp.where(low, -sn_sw, sn)


def _rope_tables(positions):
    rows = positions.size
    inv = ROPE_THETA ** (-jnp.arange(HALF, dtype=F32) / HALF)
    inv = jnp.concatenate([inv, inv]).reshape(1, HEAD_DIM)
    return pl.pallas_call(
        _rope_kernel,
        out_shape=(jax.ShapeDtypeStruct((rows, HEAD_DIM), F32),) * 2,
        grid=(rows // ROPE_TS,),
        in_specs=[
            pl.BlockSpec((ROPE_TS, 1), lambda i: (i, 0)),
            pl.BlockSpec((1, HEAD_DIM), lambda i: (0, 0)),
        ],
        out_specs=(pl.BlockSpec((ROPE_TS, HEAD_DIM), lambda i: (i, 0)),) * 2,
        compiler_params=_cparams(("arbitrary",)),
        name="rope_tables",
    )(positions.reshape(rows, 1), inv)


PROJ_TM = 512


def _inproj_kernel(x_ref, nw_ref, sh_ref, sc_ref, cos_ref, sin_ref, w_ref, wgate_ref, o_ref, g_ref):
    h = (_rms(x_ref[...], nw_ref[...]) * (1.0 + sc_ref[0]) + sh_ref[0]).astype(BF16)
    cos = cos_ref[...]
    sin = sin_ref[...]
    for c in range(PROJ_COLS // MXU_DIM):
        acc = jnp.dot(h, w_ref[:, c * MXU_DIM:(c + 1) * MXU_DIM], preferred_element_type=F32)
        for g in range(MXU_DIM // HEAD_DIM):
            grp = c * (MXU_DIM // HEAD_DIM) + g
            y = acc[:, g * HEAD_DIM:(g + 1) * HEAD_DIM]
            if grp in ROPE_GROUPS:
                y = y * cos + pltpu.roll(y, HALF, 1) * sin
            o_ref[:, grp * HEAD_DIM:(grp + 1) * HEAD_DIM] = y.astype(BF16)
    g_ref[...] = jnp.dot(h, wgate_ref[...], preferred_element_type=F32)


def _inproj(x2d, nw, mod3, cos, sin, w_in, seq):
    rows = x2d.shape[0]
    tpb = seq // PROJ_TM
    w_main = w_in[:, :PROJ_COLS].astype(BF16)
    w_gate = jnp.pad(w_in[:, PROJ_COLS:], ((0, 0), (0, PROJ_COLS_PADDED - w_in.shape[1]))).astype(BF16)
    return pl.pallas_call(
        _inproj_kernel,
        out_shape=(jax.ShapeDtypeStruct((rows, PROJ_COLS), BF16),
                   jax.ShapeDtypeStruct((rows, LANES), F32)),
        grid=(rows // PROJ_TM,),
        in_specs=[
            pl.BlockSpec((PROJ_TM, D_MODEL), lambda i: (i, 0)),
            pl.BlockSpec((1, D_MODEL), lambda i: (0, 0)),
            pl.BlockSpec((1, 1, D_MODEL), lambda i: ((i // tpb) * 6 + 0, 0, 0)),
            pl.BlockSpec((1, 1, D_MODEL), lambda i: ((i // tpb) * 6 + 1, 0, 0)),
            pl.BlockSpec((PROJ_TM, HEAD_DIM), lambda i: (i, 0)),
            pl.BlockSpec((PROJ_TM, HEAD_DIM), lambda i: (i, 0)),
            pl.BlockSpec((D_MODEL, PROJ_COLS), lambda i: (0, 0), pipeline_mode=pl.Buffered(1)),
            pl.BlockSpec((D_MODEL, LANES), lambda i: (0, 0)),
        ],
        out_specs=(pl.BlockSpec((PROJ_TM, PROJ_COLS), lambda i: (i, 0)),
                   pl.BlockSpec((PROJ_TM, LANES), lambda i: (i, 0))),
        compiler_params=_cparams(("arbitrary",)),
        name="inproj",
    )(x2d, nw, mod3, mod3, cos, sin, w_main, w_gate)


EXP2_SCALE = SCALE * 1.4426950408889634


def _lane_tiles(x):
    return [x[:, t * LANES:(t + 1) * LANES] for t in range(x.shape[1] // LANES)]


def _lane_fold(x, op):
    return functools.reduce(op, _lane_tiles(x))


def _ones_augment(v):
    return jnp.concatenate([v, jnp.ones(v.shape, v.dtype)], axis=1)


def _normalize(pv):
    return pv[:, :HEAD_DIM] * (1.0 / pv[:, HEAD_DIM:HEAD_DIM + 1])


def _softmax_pv(s, v_aug):
    m = jnp.max(s, axis=-1, keepdims=True)
    p = jnp.exp2((s - m) * EXP2_SCALE)
    return _normalize(jnp.dot(p.astype(BF16), v_aug, preferred_element_type=F32))


def _two_pass_attention(qk_fn, v_fn, last_mask_fn, n_full, s_scr, m_scr, acc_scr):
    m_scr[...] = jnp.full(m_scr.shape, NEG, F32)

    def pass1_tile(j):
        s = qk_fn(j, 1)
        s_scr[j] = s
        return _lane_fold(s, jnp.maximum)

    def pass1(pair, carry):
        s = qk_fn(pair, 2)
        tk = s.shape[1] // 2
        s_scr[2 * pair] = s[:, :tk]
        s_scr[2 * pair + 1] = s[:, tk:]
        m_scr[...] = jnp.maximum(m_scr[...], _lane_fold(s, jnp.maximum))
        return carry

    lax.fori_loop(0, n_full // 2, pass1, 0)

    @pl.when(n_full % 2 == 1)
    def _():
        m_scr[...] = jnp.maximum(m_scr[...], pass1_tile(n_full - 1))

    s = last_mask_fn(qk_fn(n_full, 1))
    s_scr[n_full] = s
    m = jnp.max(jnp.maximum(m_scr[...], _lane_fold(s, jnp.maximum)), axis=-1, keepdims=True)
    m_scr[...] = jnp.broadcast_to(m, m_scr.shape)
    acc_scr[...] = jnp.zeros(acc_scr.shape, F32)

    def pass2_tile(j):
        mb = m_scr[...]
        ps = [jnp.exp2((st - mb) * EXP2_SCALE) for st in _lane_tiles(s_scr[j])]
        p = jnp.concatenate(ps, axis=1).astype(BF16)
        return jnp.dot(p, v_fn(j, 1), preferred_element_type=F32)

    def pass2(pair, carry):
        mb = m_scr[...]
        ps = [jnp.exp2((st - mb) * EXP2_SCALE)
              for t in (2 * pair, 2 * pair + 1) for st in _lane_tiles(s_scr[t])]
        p = jnp.concatenate(ps, axis=1).astype(BF16)
        acc_scr[...] += jnp.dot(p, v_fn(pair, 2), preferred_element_type=F32)
        return carry

    n_tiles = n_full + 1
    lax.fori_loop(0, n_tiles // 2, pass2, 0)

    @pl.when(n_tiles % 2 == 1)
    def _():
        acc_scr[...] += pass2_tile(n_tiles - 1)

    return _normalize(acc_scr[...])


def _rank_desc(score, axis, count):
    idx = lax.broadcasted_iota(jnp.int32, score.shape, axis)
    rank = jnp.zeros(score.shape, F32)
    for jp in range(count):
        other = score[:, jp:jp + 1] if axis == 1 else score[jp:jp + 1, :]
        beats = (other > score) | ((other == score) & (idx > jp))
        rank = rank + jnp.where(beats, 1.0, 0.0)
    return rank


MOBA_T = 1024
MOBA_SETUP_CH = 512


def _moba_kernel(q_ref, k_ref, v_ref, wcast_in_ref, wcast2_in_ref, o_ref, wcast_out_ref, wcast2_out_ref,
                 kmean_scr, qaug_scr, kaug_scr, vaug_scr, s_scr, m_scr, acc_scr, *, nblk):
    wcast_out_ref[...] = wcast_in_ref[...].astype(BF16)
    wcast2_out_ref[...] = wcast2_in_ref[...].astype(BF16)
    qi = pl.program_id(2)
    t = MOBA_T
    blk_shift = MOBA_BLOCK.bit_length() - 1

    @pl.when(qi == 0)
    def _():
        ch = MOBA_SETUP_CH
        kcol = lax.broadcasted_iota(jnp.int32, (ch, LANES), 1)
        krow = lax.broadcasted_iota(jnp.int32, (ch, LANES), 0)
        for n in range(nblk):
            kb = k_ref[n * MOBA_BLOCK:(n + 1) * MOBA_BLOCK, :].astype(F32)
            kmean_scr[n:n + 1, :] = jnp.mean(kb, axis=0, keepdims=True)
        kmean = kmean_scr[...].astype(BF16)
        brow = lax.broadcasted_iota(jnp.int32, (nblk, ch), 0)
        qlane = lax.broadcasted_iota(jnp.int32, (nblk, ch), 1)
        for c in range(k_ref.shape[0] // ch):
            rows = slice(c * ch, (c + 1) * ch)
            kaug_scr[rows, :HEAD_DIM] = k_ref[rows, :]
            blk = jnp.right_shift(c * ch + krow, blk_shift)
            kaug_scr[rows, HEAD_DIM:] = jnp.where(kcol == blk, 1.0, 0.0).astype(BF16)
            vaug_scr[rows, :] = _ones_augment(v_ref[rows, :])
            q = q_ref[rows, :]
            qblk = jnp.right_shift(c * ch + qlane, blk_shift)
            past = brow < qblk
            gate = jnp.where(past, _dot_nt(kmean, q), NEG)
            rank = _rank_desc(gate, 0, nblk)
            allow = (past & (rank < float(MOBA_TOPK))) | (brow == qblk)
            bias_t = jnp.where(allow, 0.0, NEG)
            bias_t = jnp.concatenate([bias_t, jnp.zeros((LANES - nblk, ch), F32)], axis=0)
            qaug_scr[rows, :HEAD_DIM] = q
            qaug_scr[rows, HEAD_DIM:] = bias_t.T.astype(BF16)

    q_aug = qaug_scr[pl.ds(pl.multiple_of(qi * t, t), t), :]

    def qk(j, n):
        return _dot_nt(q_aug, kaug_scr[pl.ds(pl.multiple_of(j * (n * t), n * t), n * t), :])

    def vt(j, n):
        return vaug_scr[pl.ds(pl.multiple_of(j * (n * t), n * t), n * t), :]

    def causal(s):
        r = lax.broadcasted_iota(jnp.int32, s.shape, 0)
        c = lax.broadcasted_iota(jnp.int32, s.shape, 1)
        return jnp.where(c <= r, s, NEG)

    o = _two_pass_attention(qk, vt, causal, qi, s_scr, m_scr, acc_scr)
    o_ref[...] = o.astype(BF16)


def _moba(proj, w_cast, w_cast2, bsz, seq):
    nq = seq // MOBA_T
    nblk = seq // MOBA_BLOCK
    assert nblk % 8 == 0 and nblk <= LANES
    steps = bsz * MOBA_HEADS * nq
    step_of = lambda b, h, i: (b * MOBA_HEADS + h) * nq + i
    wrows, wcols = w_cast.shape
    wc = wcols // steps
    assert wc * steps == wcols and wc % LANES == 0
    wspec = pl.BlockSpec((wrows, wc), lambda b, h, i: (0, step_of(b, h, i)))
    wr2 = w_cast2.shape[0] // steps
    assert wr2 * steps == w_cast2.shape[0] and wr2 % 16 == 0
    wspec2 = pl.BlockSpec((wr2, w_cast2.shape[1]), lambda b, h, i: (step_of(b, h, i), 0))
    return pl.pallas_call(
        functools.partial(_moba_kernel, nblk=nblk),
        out_shape=(jax.ShapeDtypeStruct((bsz * seq, MOBA_WIDTH), BF16),
                   jax.ShapeDtypeStruct(w_cast.shape, BF16),
                   jax.ShapeDtypeStruct(w_cast2.shape, BF16)),
        grid=(bsz, MOBA_HEADS, nq),
        in_specs=[
            pl.BlockSpec((seq, HEAD_DIM), lambda b, h, i: (b, CB_MQ + h)),
            pl.BlockSpec((seq, HEAD_DIM), lambda b, h, i: (b, CB_MK + h)),
            pl.BlockSpec((seq, HEAD_DIM), lambda b, h, i: (b, CB_MV + h)),
            wspec, wspec2,
        ],
        out_specs=(pl.BlockSpec((MOBA_T, HEAD_DIM), lambda b, h, i: (b * nq + i, h)), wspec, wspec2),
        scratch_shapes=[
            pltpu.VMEM((nblk, HEAD_DIM), F32),
            pltpu.VMEM((seq, 2 * HEAD_DIM), BF16),
            pltpu.VMEM((seq, 2 * HEAD_DIM), BF16),
            pltpu.VMEM((seq, 2 * HEAD_DIM), BF16),
            pltpu.VMEM((nq, MOBA_T, MOBA_T), F32),
            pltpu.VMEM((MOBA_T, LANES), F32),
            pltpu.VMEM((MOBA_T, 2 * HEAD_DIM), F32),
        ],
        compiler_params=_cparams(("arbitrary", "arbitrary", "arbitrary")),
        name="moba",
    )(proj, proj, proj, w_cast, w_cast2)


CMP_HALF_FEAT = CMP_STRIDE * HEAD_DIM


def _compress_kernel(x_ref, pos_ref, w1_ref, w2_ref, o_ref):
    x = x_ref[0, 0]
    w1 = w1_ref[...]
    a = jnp.dot(x, w1[:CMP_HALF_FEAT, :].astype(BF16), preferred_element_type=F32)
    b = jnp.dot(x, w1[CMP_HALF_FEAT:, :].astype(BF16), preferred_element_type=F32)
    nch = x.shape[0]
    b_next = pltpu.roll(b, nch - 1, 0)
    pos_bias = jnp.sum(w1 * pos_ref[...], axis=0, keepdims=True)
    hid = jax.nn.gelu(a + b_next + pos_bias, approximate=True)
    o_ref[0, 0] = jnp.dot(hid.astype(BF16), w2_ref[...].astype(BF16),
                          preferred_element_type=F32).astype(BF16)


def _compress(xc, pos_col, w1, w2):
    bsz, hk, nch, feat = xc.shape
    return pl.pallas_call(
        _compress_kernel,
        out_shape=jax.ShapeDtypeStruct((bsz, hk, nch, HEAD_DIM), BF16),
        grid=(bsz, hk),
        in_specs=[
            pl.BlockSpec((1, 1, nch, feat), lambda b, k: (b, k, 0, 0)),
            pl.BlockSpec((CMP_LEN * HEAD_DIM, 1), lambda b, k: (0, 0)),
            pl.BlockSpec((CMP_LEN * HEAD_DIM, CMP_HIDDEN), lambda b, k: (0, 0)),
            pl.BlockSpec((CMP_HIDDEN, HEAD_DIM), lambda b, k: (0, 0)),
        ],
        out_specs=pl.BlockSpec((1, 1, nch, HEAD_DIM), lambda b, k: (b, k, 0, 0)),
        compiler_params=_cparams(("arbitrary", "arbitrary")),
        name="compress",
    )(xc, pos_col, w1, w2)


NSA_TQ = 256
NSA_TK = 512
NSA_ROWS = NSA_GROUP * NSA_TQ


def _split3(x):
    hi = x.astype(BF16)
    r1 = x - hi.astype(F32)
    mid = r1.astype(BF16)
    lo = (r1 - mid.astype(F32)).astype(BF16)
    return hi, mid, lo


def _nsa_kernel(q_ref, kc_ref, vc_ref, ks_ref, vs_ref, kw_ref, vw_ref, gl_ref, mt_ref, wb_ref,
                wcast_in_ref, o_ref, wcast_out_ref,
                kaug_scr, vsaug_scr, vwaug_scr, s_scr, m_scr, acc_scr):
    wcast_out_ref[...] = wcast_in_ref[...].astype(BF16)
    qi = pl.program_id(2)
    tq = NSA_TQ
    tk = NSA_TK
    seq = ks_ref.shape[0]
    slc_shift = SLC_BLOCK.bit_length() - 1

    @pl.when(qi == 0)
    def _():
        kcol = lax.broadcasted_iota(jnp.int32, (tk, LANES), 1)
        krow = lax.broadcasted_iota(jnp.int32, (tk, LANES), 0)
        for c in range(seq // tk):
            rows = slice(c * tk, (c + 1) * tk)
            kaug_scr[rows, :HEAD_DIM] = ks_ref[rows, :]
            blk = jnp.right_shift(c * tk + krow, slc_shift)
            kaug_scr[rows, HEAD_DIM:] = jnp.where(kcol == blk, 1.0, 0.0).astype(BF16)
            vsaug_scr[rows, :] = _ones_augment(vs_ref[rows, :])
            vwaug_scr[rows, :] = _ones_augment(vw_ref[rows, :])

    q = q_ref[...]
    qg = jnp.concatenate([q[:, g * HEAD_DIM:(g + 1) * HEAD_DIM] for g in range(NSA_GROUP)], axis=0)
    ncmp = kc_ref.shape[2]

    def qpos_of(shape):
        return qi * tq + jnp.bitwise_and(lax.broadcasted_iota(jnp.int32, shape, 0), tq - 1)

    span = WINDOW + tq
    start = pl.multiple_of(jnp.maximum(qi * tq - WINDOW, 0), tq)
    s = _dot_nt(qg, kw_ref[pl.ds(start, span), :])
    s = (s.reshape(NSA_GROUP, tq, span) + wb_ref[...]).reshape(NSA_ROWS, span)
    o_win = _softmax_pv(s, vwaug_scr[pl.ds(start, span), :])

    colc = lax.broadcasted_iota(jnp.int32, (NSA_ROWS, ncmp), 1)
    cmask = colc * CMP_STRIDE + (CMP_LEN - 1) <= qpos_of((NSA_ROWS, ncmp))
    s = jnp.where(cmask, _dot_nt(qg, kc_ref[0, 0]), NEG)
    e = jnp.exp2((s - jnp.max(s, axis=-1, keepdims=True)) * EXP2_SCALE)
    has_blk = qpos_of((NSA_ROWS, 1)) >= CMP_LEN - 1
    p = e * jnp.where(has_blk, 1.0 / jnp.sum(e, axis=-1, keepdims=True), 0.0)
    o_cmp = jnp.dot(p.astype(BF16), vc_ref[0, 0], preferred_element_type=F32)

    imp = p[0:tq]
    for g in range(1, NSA_GROUP):
        imp = imp + p[g * tq:(g + 1) * tq]
    n_slc = seq // SLC_BLOCK
    mt = mt_ref[...]
    p_slc = sum(_dot_nt(mt, part) for part in _split3(imp))
    jrow = lax.broadcasted_iota(jnp.int32, p_slc.shape, 0)
    jt = jnp.right_shift(qi * tq + lax.broadcasted_iota(jnp.int32, p_slc.shape, 1), slc_shift)
    valid = jrow <= jt
    forced = (jrow == 0) | (jrow == jt) | (jrow == jt - 1)
    score = jnp.where(valid & forced, BIG, jnp.where(valid, p_slc, NEG))
    rank = _rank_desc(score, 0, n_slc)
    sel = valid & (rank < float(min(SLC_TOPK, n_slc)))
    bias_t = jnp.where(sel, 0.0, NEG)
    bias_t = jnp.concatenate([bias_t, jnp.zeros((LANES - n_slc, tq), F32)], axis=0)
    bias = bias_t.T.astype(BF16)
    q_aug = jnp.concatenate([qg, jnp.concatenate([bias] * NSA_GROUP, axis=0)], axis=1)

    def slc_qk(j, n):
        return _dot_nt(q_aug, kaug_scr[pl.ds(pl.multiple_of(j * (n * tk), n * tk), n * tk), :])

    def slc_v(j, n):
        return vsaug_scr[pl.ds(pl.multiple_of(j * (n * tk), n * tk), n * tk), :]

    n_full = (qi * tq) // tk

    def slc_causal(s):
        kpos = n_full * tk + lax.broadcasted_iota(jnp.int32, s.shape, 1)
        return jnp.where(kpos <= qpos_of(s.shape), s, NEG)

    o_slc = _two_pass_attention(slc_qk, slc_v, slc_causal, n_full, s_scr, m_scr, acc_scr)

    sig = jax.nn.sigmoid(gl_ref[...])
    kv_head = pl.program_id(1)

    def gate_col(branch, g):
        cols = [sig[:, branch * NSA_HEADS + k * NSA_GROUP + g:branch * NSA_HEADS + k * NSA_GROUP + g + 1]
                for k in range(NSA_KV_HEADS)]
        col = cols[0]
        for k in range(1, NSA_KV_HEADS):
            col = jnp.where(kv_head == k, cols[k], col)
        return col

    for g in range(NSA_GROUP):
        rows = slice(g * tq, (g + 1) * tq)
        og = (gate_col(0, g) * o_cmp[rows] + gate_col(1, g) * o_slc[rows] + gate_col(2, g) * o_win[rows])
        o_ref[:, g * HEAD_DIM:(g + 1) * HEAD_DIM] = og.astype(BF16)


def _slc_weight_matrix(n_slc, ncmp):
    rs, rc = SLC_BLOCK // CMP_STRIDE, CMP_LEN // CMP_STRIDE
    j = jnp.arange(n_slc)[:, None]
    n = jnp.arange(ncmp)[None, :]
    w = jnp.zeros((n_slc, ncmp), F32)
    for o in range(rs + rc - 1):
        w_o = float(sum(1 for m in range(rs) for nn in range(rc) if m - nn + rc - 1 == o))
        w = w + jnp.where(n == rs * j + o - (rc - 1), w_o, 0.0)
    return w.astype(BF16)


def _window_bias():
    span = WINDOW + NSA_TQ
    cases = WINDOW // NSA_TQ + 1
    c = jnp.arange(cases)[:, None, None]
    qpos = c * NSA_TQ + jnp.arange(NSA_TQ)[None, :, None]
    kpos = jnp.maximum(c * NSA_TQ - WINDOW, 0) + jnp.arange(span)[None, None, :]
    return jnp.where((kpos <= qpos) & (kpos > qpos - WINDOW), 0.0, NEG).astype(F32)


def _nsa(proj, gates, kc, vc, w_cast, bsz, seq):
    nq = seq // NSA_TQ
    ncmp = kc.shape[2]
    n_slc = seq // SLC_BLOCK
    assert n_slc % 8 == 0 and n_slc <= LANES and seq % NSA_TK == 0
    wbias = _window_bias()
    last_case = wbias.shape[0] - 1
    qw = NSA_GROUP * HEAD_DIM
    kv_spec = lambda cb: pl.BlockSpec((seq, HEAD_DIM), lambda b, k, i: (b, cb + k))
    cmp_spec = pl.BlockSpec((1, 1, ncmp, HEAD_DIM), lambda b, k, i: (b, k, 0, 0))
    steps = bsz * NSA_KV_HEADS * nq
    wrows, wcols = w_cast.shape
    wr = wrows // steps
    assert wr * steps == wrows and wr % 16 == 0
    wspec = pl.BlockSpec((wr, wcols), lambda b, k, i: ((b * NSA_KV_HEADS + k) * nq + i, 0))
    return pl.pallas_call(
        _nsa_kernel,
        out_shape=(jax.ShapeDtypeStruct((bsz * seq, NSA_WIDTH), BF16),
                   jax.ShapeDtypeStruct(w_cast.shape, BF16)),
        grid=(bsz, NSA_KV_HEADS, nq),
        in_specs=[
            pl.BlockSpec((NSA_TQ, qw), lambda b, k, i: (b * nq + i, CB_NQ * HEAD_DIM // qw + k)),
            cmp_spec, cmp_spec,
            kv_spec(CB_KS), kv_spec(CB_VS), kv_spec(CB_KW), kv_spec(CB_VW),
            pl.BlockSpec((NSA_TQ, LANES), lambda b, k, i: (b * nq + i, 0)),
            pl.BlockSpec((n_slc, ncmp), lambda b, k, i: (0, 0)),
            pl.BlockSpec((1,) + wbias.shape[1:], lambda b, k, i: (jnp.minimum(i, last_case), 0, 0)),
            wspec,
        ],
        out_specs=(pl.BlockSpec((NSA_TQ, qw), lambda b, k, i: (b * nq + i, k)), wspec),
        scratch_shapes=[
            pltpu.VMEM((seq, 2 * HEAD_DIM), BF16),
            pltpu.VMEM((seq, 2 * HEAD_DIM), BF16),
            pltpu.VMEM((seq, 2 * HEAD_DIM), BF16),
            pltpu.VMEM((seq // NSA_TK, NSA_ROWS, NSA_TK), F32),
            pltpu.VMEM((NSA_ROWS, LANES), F32),
            pltpu.VMEM((NSA_ROWS, 2 * HEAD_DIM), F32),
        ],
        compiler_params=_cparams(("arbitrary", "arbitrary", "arbitrary")),
        name="nsa",
    )(proj, kc, vc, proj, proj, proj, proj, gates, _slc_weight_matrix(n_slc, ncmp), wbias, w_cast)


OUT_TM = 512


def _outproj_kernel(om_ref, on_ref, w_ref, x_ref, ga_ref, npost_ref, npre_ref, sh_ref, sc_ref,
                    x1_ref, h2_ref):
    o = jnp.concatenate([om_ref[...], on_ref[...]], axis=1)
    y = jnp.dot(o, w_ref[...], preferred_element_type=F32)
    x1 = x_ref[...] + ga_ref[0] * _rms(y, npost_ref[...])
    x1_ref[...] = x1
    h2_ref[...] = (_rms(x1, npre_ref[...]) * (1.0 + sc_ref[0]) + sh_ref[0]).astype(BF16)


def _outproj(o_moba, o_nsa, w_bf, x2d, mod3, npost, npre, seq):
    rows = x2d.shape[0]
    tpb = seq // OUT_TM
    mod_spec = lambda idx: pl.BlockSpec((1, 1, D_MODEL), lambda i: ((i // tpb) * 6 + idx, 0, 0))
    vec_spec = pl.BlockSpec((1, D_MODEL), lambda i: (0, 0))
    return pl.pallas_call(
        _outproj_kernel,
        out_shape=(jax.ShapeDtypeStruct((rows, D_MODEL), F32),
                   jax.ShapeDtypeStruct((rows, D_MODEL), BF16)),
        grid=(rows // OUT_TM,),
        in_specs=[
            pl.BlockSpec((OUT_TM, MOBA_WIDTH), lambda i: (i, 0)),
            pl.BlockSpec((OUT_TM, NSA_WIDTH), lambda i: (i, 0)),
            pl.BlockSpec((MOBA_WIDTH + NSA_WIDTH, D_MODEL), lambda i: (0, 0)),
            pl.BlockSpec((OUT_TM, D_MODEL), lambda i: (i, 0)),
            mod_spec(2), vec_spec, vec_spec, mod_spec(3), mod_spec(4),
        ],
        out_specs=(pl.BlockSpec((OUT_TM, D_MODEL), lambda i: (i, 0)),
                   pl.BlockSpec((OUT_TM, D_MODEL), lambda i: (i, 0))),
        compiler_params=_cparams(("arbitrary",)),
        name="outproj",
    )(o_moba, o_nsa, w_bf, x2d, mod3, npost, npre, mod3, mod3)


FFN_TM = 512
FFN_TF = 512
FFN_HALO = 16
CONV_WIDTH = 3


def _ffn_kernel(h_ref, halo_ref, wg_ref, wv_ref, cg_ref, cv_ref, wd_ref,
                x1_ref, gf_ref, npost_ref, o_ref, acc_scr, u_scr, haug_scr, *, tiles_per_seq):
    i = pl.program_id(0)
    j = pl.program_id(1)

    @pl.when(j == 0)
    def _():
        acc_scr[...] = jnp.zeros(acc_scr.shape, F32)
        halo = halo_ref[...]
        haug_scr[:FFN_HALO, :] = jnp.where(i % tiles_per_seq == 0, jnp.zeros_like(halo), halo)
        haug_scr[FFN_HALO:, :] = h_ref[...]

    h_aug = haug_scr[...]

    def conv_branch(w_ref, c_ref):
        u_scr[...] = jnp.dot(h_aug, w_ref[...], preferred_element_type=F32)
        c = c_ref[...]
        out = c[CONV_WIDTH:CONV_WIDTH + 1, :]
        for tap in range(CONV_WIDTH):
            ofs = FFN_HALO - (CONV_WIDTH - 1) + tap
            out = out + c[tap:tap + 1, :] * u_scr[pl.ds(ofs, FFN_TM), :]
        return out

    gate = conv_branch(wg_ref, cg_ref)
    val = conv_branch(wv_ref, cv_ref)
    act = (jax.nn.gelu(gate, approximate=True) * val).astype(BF16)
    acc_scr[...] += jnp.dot(act, wd_ref[...], preferred_element_type=F32)

    @pl.when(j == pl.num_programs(1) - 1)
    def _():
        o_ref[...] = x1_ref[...] + gf_ref[0] * _rms(acc_scr[...], npost_ref[...])


def _ffn(h2, x1, w_up_bf, conv_w, conv_b, w_down_bf, mod3, npost, seq):
    rows = h2.shape[0]
    tps = seq // FFN_TM
    nj = D_FF // FFN_TF
    conv_p = jnp.concatenate([conv_w, conv_b.reshape(1, 2 * D_FF)], axis=0)
    return pl.pallas_call(
        functools.partial(_ffn_kernel, tiles_per_seq=tps),
        out_shape=jax.ShapeDtypeStruct((rows, D_MODEL), F32),
        grid=(rows // FFN_TM, nj),
        in_specs=[
            pl.BlockSpec((FFN_TM, D_MODEL), lambda i, j: (i, 0)),
            pl.BlockSpec((FFN_HALO, D_MODEL),
                         lambda i, j: (jnp.maximum(i * (FFN_TM // FFN_HALO) - 1, 0), 0)),
            pl.BlockSpec((D_MODEL, FFN_TF), lambda i, j: (0, j)),
            pl.BlockSpec((D_MODEL, FFN_TF), lambda i, j: (0, nj + j)),
            pl.BlockSpec((CONV_WIDTH + 1, FFN_TF), lambda i, j: (0, j)),
            pl.BlockSpec((CONV_WIDTH + 1, FFN_TF), lambda i, j: (0, nj + j)),
            pl.BlockSpec((FFN_TF, D_MODEL), lambda i, j: (j, 0)),
            pl.BlockSpec((FFN_TM, D_MODEL), lambda i, j: (i, 0)),
            pl.BlockSpec((1, 1, D_MODEL), lambda i, j: ((i // tps) * 6 + 5, 0, 0)),
            pl.BlockSpec((1, D_MODEL), lambda i, j: (0, 0)),
        ],
        out_specs=pl.BlockSpec((FFN_TM, D_MODEL), lambda i, j: (i, 0)),
        scratch_shapes=[
            pltpu.VMEM((FFN_TM, D_MODEL), F32),
            pltpu.VMEM((FFN_HALO + FFN_TM, FFN_TF), F32),
            pltpu.VMEM((FFN_HALO + FFN_TM, D_MODEL), BF16),
        ],
        compiler_params=_cparams(("arbitrary", "arbitrary")),
        name="ffn",
    )(h2, h2, w_up_bf, w_up_bf, conv_p, conv_p, w_down_bf, x1, mod3, npost)


def _chunks(proj, cb, bsz, seq):
    a = proj[:, cb * HEAD_DIM:(cb + NSA_KV_HEADS) * HEAD_DIM]
    a = a.reshape(bsz, seq // CMP_STRIDE, CMP_STRIDE, NSA_KV_HEADS, HEAD_DIM).transpose(0, 3, 1, 2, 4)
    return a.reshape(bsz, NSA_KV_HEADS, seq // CMP_STRIDE, CMP_HALF_FEAT)


def kernel(x, c, positions, w_ada, b_ada, norm_pre_mix, norm_post_mix, norm_pre_ffn, norm_post_ffn,
           w_in, w_out, cmp_pos_k, cmp_w1_k, cmp_w2_k, cmp_pos_v, cmp_w1_v, cmp_w2_v,
           w_up, conv_w, conv_b, w_down):
    bsz, seq, d = x.shape
    depth = w_in.shape[0]
    assert d == D_MODEL and seq % WINDOW == 0 and seq % PROJ_TM == 0
    cos, sin = _rope_tables(positions)
    x2d = x.reshape(bsz * seq, d)
    for l in range(depth):
        mod3 = _ada(c, w_ada[l], b_ada[l]).reshape(bsz * 6, 1, d)
        assert w_in.shape[2] == PROJ_COLS + 3 * NSA_HEADS
        proj, gates = _inproj(x2d, norm_pre_mix[l].reshape(1, d), mod3, cos, sin, w_in[l], seq)
        o_moba, w_up_bf, w_out_bf = _moba(proj, w_up[l], w_out[l], bsz, seq)
        kc = _compress(_chunks(proj, CB_KC, bsz, seq), cmp_pos_k[l].reshape(-1, 1), cmp_w1_k[l], cmp_w2_k[l])
        vc = _compress(_chunks(proj, CB_VC, bsz, seq), cmp_pos_v[l].reshape(-1, 1), cmp_w1_v[l], cmp_w2_v[l])
        o_nsa, w_down_bf = _nsa(proj, gates, kc, vc, w_down[l], bsz, seq)
        x1, h2 = _outproj(o_moba, o_nsa, w_out_bf, x2d, mod3, norm_post_mix[l].reshape(1, d),
                          norm_pre_ffn[l].reshape(1, d), seq)
        x2d = _ffn(h2, x1, w_up_bf, conv_w[l], conv_b[l], w_down_bf, mod3,
                   norm_post_ffn[l].reshape(1, d), seq)
    return x2d.reshape(bsz, seq, d)
```
